```python
import jax, jax.numpy as jnp
from jax import lax
import numpy as np

D_MODEL = 1024
BATCH = 4
SEQ = 8192
DEPTH = 1

N_HEADS = 8
HEAD_DIM = 64
N_KV_HEADS = 2
GQA_GROUP = N_HEADS // N_KV_HEADS
ATTN_WIDTH = N_HEADS * HEAD_DIM
WINDOW = 128
BLOCK = 128
CONV_WIDTH = D_MODEL - ATTN_WIDTH
CONV_GROUPS = 8
CONV_K = 3
MIX_WIDTH = ATTN_WIDTH + CONV_WIDTH
KV_WIDTH = N_KV_HEADS * HEAD_DIM
IN_COLS = ATTN_WIDTH + 2 * KV_WIDTH + 3 * CONV_WIDTH
D_FF = 4 * D_MODEL
EPS = 1e-6
NEG_INF = -1e30

kernel_name = "hymba_swa_sink_shortconv_relu2"


def rms_norm(x, g):
    xf = x.astype(jnp.float32)
    y = xf * lax.rsqrt(jnp.mean(xf * xf, axis=-1, keepdims=True) + EPS)
    return (y * g.astype(jnp.float32)).astype(x.dtype)


def swa_sink_attention(q, k, v, sinks):
    b, s = q.shape[0], q.shape[1]
    nb = s // BLOCK
    qb = q.reshape(b, nb, BLOCK, N_KV_HEADS, GQA_GROUP, HEAD_DIM)
    pad = ((0, 0), (BLOCK, 0), (0, 0), (0, 0))
    kp = jnp.pad(k, pad).reshape(b, nb + 1, BLOCK, N_KV_HEADS, HEAD_DIM)
    vp = jnp.pad(v, pad).reshape(b, nb + 1, BLOCK, N_KV_HEADS, HEAD_DIM)
    kw = jnp.concatenate([kp[:, :-1], kp[:, 1:]], axis=2)
    vw = jnp.concatenate([vp[:, :-1], vp[:, 1:]], axis=2)
    scale = HEAD_DIM ** -0.5
    sc = jnp.einsum('bnqhgd,bnkhd->bnhgqk', qb, kw).astype(jnp.float32) * scale
    blk = jnp.arange(nb)[:, None] * BLOCK
    qpos = blk + jnp.arange(BLOCK)[None, :]
    kpos = blk - BLOCK + jnp.arange(2 * BLOCK)[None, :]
    delta = qpos[:, :, None] - kpos[:, None, :]
    mask = (delta >= 0) & (delta < WINDOW) & (kpos[:, None, :] >= 0)
    sc = jnp.where(mask[None, :, None, None], sc, NEG_INF)
    sink = jnp.broadcast_to(sinks.astype(jnp.float32).reshape(1, 1, N_KV_HEADS, GQA_GROUP, 1, 1),
                            sc.shape[:-1] + (1,))
    p = jax.nn.softmax(jnp.concatenate([sc, sink], axis=-1), axis=-1)[..., :-1]
    o = jnp.einsum('bnhgqk,bnkhd->bnqhgd', p.astype(v.dtype), vw)
    return o.reshape(b, s, ATTN_WIDTH)


def short_conv(u, w):
    s = u.shape[1]
    up = jnp.pad(u, ((0, 0), (CONV_K - 1, 0), (0, 0)))
    out = w[0] * up[:, 0:s]
    for i in range(1, CONV_K):
        out = out + w[i] * up[:, i:i + s]
    return out


def setup_inputs(seed: int = 0) -> dict:
    key = jax.random.key(seed)
    ks = jax.random.split(key, 14)
    f32 = jnp.float32
    def nrm(k, shape, scale):
        return jax.random.normal(k, shape, f32) * scale
    return {
        "x": nrm(ks[0], (BATCH, SEQ, D_MODEL), 1.0),
        "attn_norm_g": 1.0 + nrm(ks[1], (DEPTH, D_MODEL), 0.02),
        "w_in": nrm(ks[2], (DEPTH, D_MODEL, IN_COLS), D_MODEL ** -0.5),
        "q_norm_g": 1.0 + nrm(ks[3], (DEPTH, HEAD_DIM), 0.02),
        "k_norm_g": 1.0 + nrm(ks[4], (DEPTH, HEAD_DIM), 0.02),
        "sinks": nrm(ks[5], (DEPTH, N_HEADS), 0.5),
        "conv_w": nrm(ks[6], (DEPTH, CONV_K, CONV_WIDTH), CONV_K ** -0.5),
        "attn_out_g": 1.0 + nrm(ks[7], (DEPTH, ATTN_WIDTH), 0.02),
        "conv_out_g": 1.0 + nrm(ks[8], (DEPTH, CONV_WIDTH), 0.02),
        "w_out": nrm(ks[9], (DEPTH, MIX_WIDTH, D_MODEL), MIX_WIDTH ** -0.5),
        "mlp_norm_g": 1.0 + nrm(ks[10], (DEPTH, D_MODEL), 0.02),
        "w_up": nrm(ks[11], (DEPTH, D_MODEL, D_FF), D_MODEL ** -0.5),
        "w_down": nrm(ks[12], (DEPTH, D_FF, D_MODEL), D_FF ** -0.5),
    }


def reference(x, attn_norm_g, w_in, q_norm_g, k_norm_g, sinks, conv_w,
              attn_out_g, conv_out_g, w_out, mlp_norm_g, w_up, w_down):
    b, s, _ = x.shape
    for l in range(DEPTH):
        h = rms_norm(x, attn_norm_g[l])
        z = h @ w_in[l]
        o0 = ATTN_WIDTH
        o1 = o0 + KV_WIDTH
        o2 = o1 + KV_WIDTH
        o3 = o2 + CONV_WIDTH
        o4 = o3 + CONV_WIDTH
        q = z[..., :o0].reshape(b, s, N_HEADS, HEAD_DIM)
        k = z[..., o0:o1].reshape(b, s, N_KV_HEADS, HEAD_DIM)
        v = z[..., o1:o2].reshape(b, s, N_KV_HEADS, HEAD_DIM)
        gate_b = z[..., o2:o3]
        gate_c = z[..., o3:o4]
        xc = z[..., o4:]
        q = rms_norm(q, q_norm_g[l])
        k = rms_norm(k, k_norm_g[l])
        y_attn = swa_sink_attention(q, k, v, sinks[l])
        y_conv = gate_b * short_conv(gate_c * xc, conv_w[l])
        y = jnp.concatenate([rms_norm(y_attn, attn_out_g[l]),
                             rms_norm(y_conv, conv_out_g[l])], axis=-1)
        x = x + y @ w_out[l]
        hm = rms_norm(x, mlp_norm_g[l])
        x = x + jnp.square(jax.nn.relu(hm @ w_up[l])) @ w_down[l]
    return x
```

```python
import functools

import jax
import jax.numpy as jnp
from jax import lax
from jax.experimental import pallas as pl
from jax.experimental.pallas import tpu as pltpu

D_MODEL = 1024
N_HEADS = 8
HEAD_DIM = 64
N_KV_HEADS = 2
ATTN_WIDTH = N_HEADS * HEAD_DIM
KV_WIDTH = N_KV_HEADS * HEAD_DIM
CONV_WIDTH = D_MODEL - ATTN_WIDTH
CONV_K = 3
WINDOW = 128
IN_COLS = ATTN_WIDTH + 2 * KV_WIDTH + 3 * CONV_WIDTH
D_FF = 4 * D_MODEL
EPS = 1e-6
NEG_INF = -1e30

LANES = 128
SUBLANES = 8
TOKEN_TILE = 512
FF_CHUNK = 1024
VMEM_LIMIT_BYTES = 56 * 1024 * 1024

GROUP_A_HEADS = (0, 2, 5, 7)
GROUP_B_HEADS = (1, 3, 4, 6)


def _const_spec(shape):
    return pl.BlockSpec(shape, lambda *_: (0,) * len(shape), pipeline_mode=pl.Buffered(1))


def _rms_scale(xf, width):
    return lax.rsqrt(jnp.sum(xf * xf, axis=-1, keepdims=True) * (1.0 / width) + EPS)


def _in_proj_kernel(x_ref, g_ref, w_ref, gq_ref, gk_ref, eq_ref, ek_ref,
                    q_ref, kv_ref, bg_ref, u_ref):
    x = x_ref[...]
    h = (x * _rms_scale(x, D_MODEL) * g_ref[...]).astype(jnp.bfloat16)
    z = jnp.dot(h, w_ref[...], preferred_element_type=jnp.float32)
    o0 = ATTN_WIDTH
    o1 = o0 + KV_WIDTH
    o2 = o1 + KV_WIDTH
    o3 = o2 + CONV_WIDTH
    o4 = o3 + CONV_WIDTH
    q = z[:, :o0]
    k = z[:, o0:o1]
    v = z[:, o1:o2]
    q_ms = jnp.dot((q * q).astype(jnp.bfloat16), eq_ref[...], preferred_element_type=jnp.float32)
    k_ms = jnp.dot((k * k).astype(jnp.bfloat16), ek_ref[...], preferred_element_type=jnp.float32)
    qn = q * lax.rsqrt(q_ms + EPS) * gq_ref[...]
    kn = k * lax.rsqrt(k_ms + EPS) * gk_ref[...]
    q_ref[...] = qn.astype(jnp.bfloat16)
    kv_ref[:, 0 * LANES:1 * LANES] = kn.astype(jnp.bfloat16)
    kv_ref[:, 1 * LANES:2 * LANES] = pltpu.roll(kn, HEAD_DIM, 1).astype(jnp.bfloat16)
    kv_ref[:, 2 * LANES:3 * LANES] = v.astype(jnp.bfloat16)
    kv_ref[:, 3 * LANES:4 * LANES] = pltpu.roll(v, HEAD_DIM, 1).astype(jnp.bfloat16)
    bg_ref[...] = z[:, o2:o3].astype(jnp.bfloat16)
    u_ref[...] = (z[:, o3:o4] * z[:, o4:]).astype(jnp.bfloat16)


def _in_proj(x2, g, w_in, gq, gk, eq, ek):
    t = x2.shape[0]
    tm = TOKEN_TILE
    row = lambda width: pl.BlockSpec((tm, width), lambda i: (i, 0))
    out_shape = (
        jax.ShapeDtypeStruct((t, ATTN_WIDTH), jnp.bfloat16),
        jax.ShapeDtypeStruct((t, 4 * LANES), jnp.bfloat16),
        jax.ShapeDtypeStruct((t, CONV_WIDTH), jnp.bfloat16),
        jax.ShapeDtypeStruct((t, CONV_WIDTH), jnp.bfloat16),
    )
    return pl.pallas_call(
        _in_proj_kernel,
        out_shape=out_shape,
        grid=(t // tm,),
        in_specs=[
            row(D_MODEL),
            _const_spec((1, D_MODEL)),
            _const_spec((D_MODEL, IN_COLS)),
            _const_spec((1, ATTN_WIDTH)),
            _const_spec((1, KV_WIDTH)),
            _const_spec((ATTN_WIDTH, ATTN_WIDTH)),
            _const_spec((KV_WIDTH, KV_WIDTH)),
        ],
        out_specs=(row(ATTN_WIDTH), row(4 * LANES), row(CONV_WIDTH), row(CONV_WIDTH)),
        compiler_params=pltpu.CompilerParams(
            dimension_semantics=("arbitrary",), vmem_limit_bytes=VMEM_LIMIT_BYTES),
        name="in_proj",
    )(x2, g, w_in, gq, gk, eq, ek)


def _mixer_kernel(tiles_per_seq, sinks_ref, x_ref, q_ref, kv_ref, kvp_ref, bg_ref, u_ref,
                  up_ref, cw_ref, ga_ref, gc_ref, wo_ref, o_ref, kvbuf, ubuf, ybuf):
    tq = TOKEN_TILE
    first = (pl.program_id(0) % tiles_per_seq) == 0

    kvbuf[0:WINDOW, :] = kvp_ref[...]
    kvbuf[WINDOW:WINDOW + tq, :] = kv_ref[...]

    lane = lax.broadcasted_iota(jnp.int32, (WINDOW, LANES), 1)
    lo = lane < HEAD_DIM
    qi = lax.broadcasted_iota(jnp.int32, (WINDOW, 2 * WINDOW), 0)
    kj = lax.broadcasted_iota(jnp.int32, (WINDOW, 2 * WINDOW), 1)
    band = (kj > qi) & (kj <= qi + WINDOW)
    band_first = band & (kj >= jnp.where(first, WINDOW, 0))
    band4 = jnp.concatenate([band] * 4, axis=0)
    band4_first = jnp.concatenate([band_first] * 4, axis=0)

    def sink_col(heads):
        return jnp.concatenate(
            [jnp.full((WINDOW, 1), sinks_ref[h], jnp.float32) for h in heads], axis=0)

    sink_a = sink_col(GROUP_A_HEADS)
    sink_b = sink_col(GROUP_B_HEADS)
    zero = jnp.zeros((WINDOW, LANES), jnp.bfloat16)

    def attend(qs, k, v, sink, mask):
        s = lax.dot_general(qs, k, (((1,), (1,)), ((), ())),
                            preferred_element_type=jnp.float32)
        s = jnp.where(mask, s, NEG_INF)
        m = jnp.maximum(jnp.max(s, axis=-1, keepdims=True), sink)
        p = jnp.exp(s - m)
        l = jnp.sum(p, axis=-1, keepdims=True) + jnp.exp(sink - m)
        o = jnp.dot(p.astype(jnp.bfloat16), v, preferred_element_type=jnp.float32)
        return o / l

    for j in range(tq // WINDOW):
        r0 = j * WINDOW
        qb = q_ref[r0:r0 + WINDOW, :]
        slab = [qb[:, s * LANES:(s + 1) * LANES] for s in range(4)]
        q_a = jnp.concatenate([jnp.where(lo, slab[0], zero), jnp.where(lo, slab[1], zero),
                               jnp.where(lo, zero, slab[2]), jnp.where(lo, zero, slab[3])], axis=0)
        q_b = jnp.concatenate([jnp.where(lo, zero, slab[0]), jnp.where(lo, zero, slab[1]),
                               jnp.where(lo, slab[2], zero), jnp.where(lo, slab[3], zero)], axis=0)
        kv = kvbuf[r0:r0 + 2 * WINDOW, :]
        mask = band4_first if j == 0 else band4
        o_a = attend(q_a, kv[:, 0:LANES], kv[:, 2 * LANES:3 * LANES], sink_a, mask)
        o_b = attend(q_b, kv[:, LANES:2 * LANES], kv[:, 3 * LANES:4 * LANES], sink_b, mask)
        rows = lambda o, s: o[s * WINDOW:(s + 1) * WINDOW]
        y = jnp.concatenate([jnp.where(lo, rows(o_a, 0), rows(o_b, 0)),
                             jnp.where(lo, rows(o_a, 1), rows(o_b, 1)),
                             jnp.where(lo, rows(o_b, 2), rows(o_a, 2)),
                             jnp.where(lo, rows(o_b, 3), rows(o_a, 3))], axis=1)
        y = y * _rms_scale(y, ATTN_WIDTH) * ga_ref[...]
        ybuf[r0:r0 + WINDOW, 0:ATTN_WIDTH] = y.astype(jnp.bfloat16)

    u_prev = jnp.where(first, 0.0, up_ref[...].astype(jnp.float32))
    ubuf[0:SUBLANES, :] = u_prev
    u = u_ref[...].astype(jnp.float32)
    ubuf[SUBLANES:SUBLANES + tq, :] = u
    conv = (cw_ref[0:1, :] * ubuf[SUBLANES - 2:SUBLANES - 2 + tq, :]
            + cw_ref[1:2, :] * ubuf[SUBLANES - 1:SUBLANES - 1 + tq, :]
            + cw_ref[2:3, :] * u)
    yc = bg_ref[...].astype(jnp.float32) * conv
    yc = yc * _rms_scale(yc, CONV_WIDTH) * gc_ref[...]
    ybuf[:, ATTN_WIDTH:] = yc.astype(jnp.bfloat16)

    o_ref[...] = x_ref[...] + jnp.dot(ybuf[...], wo_ref[...], preferred_element_type=jnp.float32)


def _mixer(x2, q, kv, bg, u, sinks, conv_w, ga, gc, w_out, seq_len):
    t = x2.shape[0]
    tq = TOKEN_TILE
    tiles_per_seq = seq_len // tq
    row = lambda width: pl.BlockSpec((tq, width), lambda i, *_: (i, 0))
    prev_kv = pl.BlockSpec(
        (WINDOW, 4 * LANES), lambda i, *_: (jnp.maximum(i * (tq // WINDOW) - 1, 0), 0))
    prev_u = pl.BlockSpec(
        (SUBLANES, CONV_WIDTH), lambda i, *_: (jnp.maximum(i * (tq // SUBLANES) - 1, 0), 0))
    const = lambda shape: pl.BlockSpec(
        shape, lambda i, *_: (0,) * len(shape), pipeline_mode=pl.Buffered(1))
    grid_spec = pltpu.PrefetchScalarGridSpec(
        num_scalar_prefetch=1,
        grid=(t // tq,),
        in_specs=[
            row(D_MODEL), row(ATTN_WIDTH), row(4 * LANES), prev_kv, row(CONV_WIDTH),
            row(CONV_WIDTH), prev_u,
            const((CONV_K, CONV_WIDTH)), const((1, ATTN_WIDTH)), const((1, CONV_WIDTH)),
            const((D_MODEL, D_MODEL)),
        ],
        out_specs=row(D_MODEL),
        scratch_shapes=[
            pltpu.VMEM((WINDOW + tq, 4 * LANES), jnp.bfloat16),
            pltpu.VMEM((SUBLANES + tq, CONV_WIDTH), jnp.float32),
            pltpu.VMEM((tq, D_MODEL), jnp.bfloat16),
        ],
    )
    return pl.pallas_call(
        functools.partial(_mixer_kernel, tiles_per_seq),
        out_shape=jax.ShapeDtypeStruct((t, D_MODEL), jnp.float32),
        grid_spec=grid_spec,
        compiler_params=pltpu.CompilerParams(
            dimension_semantics=("arbitrary",), vmem_limit_bytes=VMEM_LIMIT_BYTES),
        name="mixer",
    )(sinks, x2, q, kv, kv, bg, u, u, conv_w, ga, gc, w_out)


def _mlp_kernel(x_ref, g_ref, wu_ref, wd_ref, o_ref):
    x = x_ref[...]
    h = (x * _rms_scale(x, D_MODEL) * g_ref[...]).astype(jnp.bfloat16)
    o_ref[...] = x
    for c in range(D_FF // FF_CHUNK):
        c0 = c * FF_CHUNK
        a = jnp.dot(h, wu_ref[:, c0:c0 + FF_CHUNK], preferred_element_type=jnp.float32)
        a = jnp.maximum(a, 0.0)
        a = (a * a).astype(jnp.bfloat16)
        o_ref[...] += jnp.dot(a, wd_ref[c0:c0 + FF_CHUNK, :], preferred_element_type=jnp.float32)


def _mlp(x2, g, w_up, w_down):
    t = x2.shape[0]
    tm = TOKEN_TILE
    row = pl.BlockSpec((tm, D_MODEL), lambda i: (i, 0))
    return pl.pallas_call(
        _mlp_kernel,
        out_shape=jax.ShapeDtypeStruct((t, D_MODEL), jnp.float32),
        grid=(t // tm,),
        in_specs=[row, _const_spec((1, D_MODEL)), _const_spec((D_MODEL, D_FF)),
                  _const_spec((D_FF, D_MODEL))],
        out_specs=row,
        compiler_params=pltpu.CompilerParams(
            dimension_semantics=("arbitrary",), vmem_limit_bytes=VMEM_LIMIT_BYTES),
        name="mlp",
    )(x2, g, w_up, w_down)


def _head_mean_matrix(width):
    head = jnp.arange(width) // HEAD_DIM
    return jnp.where(head[:, None] == head[None, :], 1.0 / HEAD_DIM, 0.0).astype(jnp.bfloat16)


def kernel(x, attn_norm_g, w_in, q_norm_g, k_norm_g, sinks, conv_w, attn_out_g, conv_out_g,
           w_out, mlp_norm_g, w_up, w_down):
    b, s, d = x.shape
    depth = w_in.shape[0]
    assert d == D_MODEL and s % TOKEN_TILE == 0
    bf16 = jnp.bfloat16
    x2 = x.reshape(b * s, d)
    eq = _head_mean_matrix(ATTN_WIDTH)
    ek = _head_mean_matrix(KV_WIDTH)
    for l in range(depth):
        gq = jnp.tile(q_norm_g[l], N_HEADS)[None, :] * (HEAD_DIM ** -0.5)
        gk = jnp.tile(k_norm_g[l], N_KV_HEADS)[None, :]
        q, kv, bg, u = _in_proj(x2, attn_norm_g[l][None, :], w_in[l].astype(bf16), gq, gk, eq, ek)
        x2 = _mixer(x2, q, kv, bg, u, sinks[l], conv_w[l], attn_out_g[l][None, :],
                    conv_out_g[l][None, :], w_out[l].astype(bf16), s)
        x2 = _mlp(x2, mlp_norm_g[l][None, :], w_up[l].astype(bf16), w_down[l].astype(bf16))
    return x2.reshape(b, s, d)
```

```python
import functools

import jax
import jax.numpy as jnp
from jax import lax
from jax.experimental import pallas as pl
from jax.experimental.pallas import tpu as pltpu

D_MODEL = 1024
N_HEADS = 8
HEAD_DIM = 64
N_KV_HEADS = 2
ATTN_WIDTH = N_HEADS * HEAD_DIM
KV_WIDTH = N_KV_HEADS * HEAD_DIM
CONV_WIDTH = D_MODEL - ATTN_WIDTH
CONV_K = 3
WINDOW = 128
IN_COLS = ATTN_WIDTH + 2 * KV_WIDTH + 3 * CONV_WIDTH
D_FF = 4 * D_MODEL
EPS = 1e-6
NEG_INF = -1e30

LANES = 128
SUBLANES = 8
TOKEN_TILE = 512
FF_CHUNK = 1024
VMEM_LIMIT_BYTES = 56 * 1024 * 1024

GROUP_A_HEADS = (0, 2, 5, 7)
GROUP_B_HEADS = (1, 3, 4, 6)


def _const_spec(shape):
    return pl.BlockSpec(shape, lambda *_: (0,) * len(shape), pipeline_mode=pl.Buffered(1))


def _rms_scale(xf, width):
    return lax.rsqrt(jnp.sum(xf * xf, axis=-1, keepdims=True) * (1.0 / width) + EPS)


def _in_proj_kernel(x_ref, g_ref, w_ref, gq_ref, gk_ref, eq_ref, ek_ref,
                    q_ref, kv_ref, bg_ref, u_ref):
    x = x_ref[...]
    h = (x * _rms_scale(x, D_MODEL) * g_ref[...]).astype(jnp.bfloat16)
    z = jnp.dot(h, w_ref[...], preferred_element_type=jnp.float32)
    o0 = ATTN_WIDTH
    o1 = o0 + KV_WIDTH
    o2 = o1 + KV_WIDTH
    o3 = o2 + CONV_WIDTH
    o4 = o3 + CONV_WIDTH
    q = z[:, :o0]
    k = z[:, o0:o1]
    v = z[:, o1:o2]
    q_ms = jnp.dot((q * q).astype(jnp.bfloat16), eq_ref[...], preferred_element_type=jnp.float32)
    k_ms = jnp.dot((k * k).astype(jnp.bfloat16), ek_ref[...], preferred_element_type=jnp.float32)
    qn = q * lax.rsqrt(q_ms + EPS) * gq_ref[...]
    kn = k * lax.rsqrt(k_ms + EPS) * gk_ref[...]
    q_ref[...] = qn.astype(jnp.bfloat16)
    kv_ref[:, 0 * LANES:1 * LANES] = kn.astype(jnp.bfloat16)
    kv_ref[:, 1 * LANES:2 * LANES] = pltpu.roll(kn, HEAD_DIM, 1).astype(jnp.bfloat16)
    kv_ref[:, 2 * LANES:3 * LANES] = v.astype(jnp.bfloat16)
    kv_ref[:, 3 * LANES:4 * LANES] = pltpu.roll(v, HEAD_DIM, 1).astype(jnp.bfloat16)
    bg_ref[...] = z[:, o2:o3].astype(jnp.bfloat16)
    u_ref[...] = (z[:, o3:o4] * z[:, o4:]).astype(jnp.bfloat16)


def _in_proj(x2, g, w_in, gq, gk, eq, ek):
    t = x2.shape[0]
    tm = TOKEN_TILE
    row = lambda width: pl.BlockSpec((tm, width), lambda i: (i, 0))
    out_shape = (
        jax.ShapeDtypeStruct((t, ATTN_WIDTH), jnp.bfloat16),
        jax.ShapeDtypeStruct((t, 4 * LANES), jnp.bfloat16),
        jax.ShapeDtypeStruct((t, CONV_WIDTH), jnp.bfloat16),
        jax.ShapeDtypeStruct((t, CONV_WIDTH), jnp.bfloat16),
    )
    return pl.pallas_call(
        _in_proj_kernel,
        out_shape=out_shape,
        grid=(t // tm,),
        in_specs=[
            row(D_MODEL),
            _const_spec((1, D_MODEL)),
            _const_spec((D_MODEL, IN_COLS)),
            _const_spec((1, ATTN_WIDTH)),
            _const_spec((1, KV_WIDTH)),
            _const_spec((ATTN_WIDTH, ATTN_WIDTH)),
            _const_spec((KV_WIDTH, KV_WIDTH)),
        ],
        out_specs=(row(ATTN_WIDTH), row(4 * LANES), row(CONV_WIDTH), row(CONV_WIDTH)),
        compiler_params=pltpu.CompilerParams(
            dimension_semantics=("arbitrary",), vmem_limit_bytes=VMEM_LIMIT_BYTES),
        name="in_proj",
    )(x2, g, w_in, gq, gk, eq, ek)


def _mixer_kernel(tiles_per_seq, sinks_ref, x_ref, q_ref, kv_ref, kvp_ref, bg_ref, u_ref,
                  up_ref, cw_ref, ga_ref, gc_ref, wo_ref, o_ref, ubuf, ybuf):
    tq = TOKEN_TILE
    first = (pl.program_id(0) % tiles_per_seq) == 0

    lane = lax.broadcasted_iota(jnp.int32, (WINDOW, LANES), 1)
    lo = lane < HEAD_DIM
    qi = lax.broadcasted_iota(jnp.int32, (WINDOW, 2 * WINDOW), 0)
    kj = lax.broadcasted_iota(jnp.int32, (WINDOW, 2 * WINDOW), 1)
    band = (kj > qi) & (kj <= qi + WINDOW)
    band_first = band & (kj >= jnp.where(first, WINDOW, 0))
    band4 = jnp.concatenate([band] * 4, axis=0)
    band4_first = jnp.concatenate([band_first] * 4, axis=0)

    def sink_rows(heads):
        return jnp.concatenate(
            [jnp.full((WINDOW, LANES), sinks_ref[h], jnp.float32) for h in heads], axis=0)

    sink_a = sink_rows(GROUP_A_HEADS)
    sink_b = sink_rows(GROUP_B_HEADS)
    zero = jnp.zeros((WINDOW, LANES), jnp.bfloat16)
    ones = jnp.ones((2 * WINDOW, LANES), jnp.bfloat16)

    def attend(qs, k, v, sink, mask):
        s = lax.dot_general(qs, k, (((1,), (1,)), ((), ())),
                            preferred_element_type=jnp.float32)
        s = jnp.where(mask, s, NEG_INF)
        m = jnp.max(jnp.maximum(s[:, :LANES], s[:, LANES:]), axis=-1, keepdims=True)
        m = jnp.maximum(jnp.broadcast_to(m, sink.shape), sink)
        p = jnp.exp(s - jnp.concatenate([m, m], axis=1))
        ov = jnp.dot(p.astype(jnp.bfloat16), jnp.concatenate([v, ones], axis=1),
                     preferred_element_type=jnp.float32)
        l = ov[:, LANES:] + jnp.exp(sink - m)
        return ov[:, :LANES] / l

    for j in range(tq // WINDOW):
        r0 = j * WINDOW
        qb = q_ref[r0:r0 + WINDOW, :]
        slab = [qb[:, s * LANES:(s + 1) * LANES] for s in range(4)]
        q_a = jnp.concatenate([jnp.where(lo, slab[0], zero), jnp.where(lo, slab[1], zero),
                               jnp.where(lo, zero, slab[2]), jnp.where(lo, zero, slab[3])], axis=0)
        q_b = jnp.concatenate([jnp.where(lo, zero, slab[0]), jnp.where(lo, zero, slab[1]),
                               jnp.where(lo, slab[2], zero), jnp.where(lo, slab[3], zero)], axis=0)
        if j == 0:
            kv = jnp.concatenate([kvp_ref[...], kv_ref[0:WINDOW, :]], axis=0)
        else:
            kv = kv_ref[r0 - WINDOW:r0 + WINDOW, :]
        mask = band4_first if j == 0 else band4
        o_a = attend(q_a, kv[:, 0:LANES], kv[:, 2 * LANES:3 * LANES], sink_a, mask)
        o_b = attend(q_b, kv[:, LANES:2 * LANES], kv[:, 3 * LANES:4 * LANES], sink_b, mask)
        rows = lambda o, s: o[s * WINDOW:(s + 1) * WINDOW]
        y = jnp.concatenate([jnp.where(lo, rows(o_a, 0), rows(o_b, 0)),
                             jnp.where(lo, rows(o_a, 1), rows(o_b, 1)),
                             jnp.where(lo, rows(o_b, 2), rows(o_a, 2)),
                             jnp.where(lo, rows(o_b, 3), rows(o_a, 3))], axis=1)
        y = y * _rms_scale(y, ATTN_WIDTH) * ga_ref[...]
        ybuf[r0:r0 + WINDOW, 0:ATTN_WIDTH] = y.astype(jnp.bfloat16)

    u_prev = jnp.where(first, 0.0, up_ref[...].astype(jnp.float32))
    ubuf[0:SUBLANES, :] = u_prev
    u = u_ref[...].astype(jnp.float32)
    ubuf[SUBLANES:SUBLANES + tq, :] = u
    conv = (cw_ref[0:1, :] * ubuf[SUBLANES - 2:SUBLANES - 2 + tq, :]
            + cw_ref[1:2, :] * ubuf[SUBLANES - 1:SUBLANES - 1 + tq, :]
            + cw_ref[2:3, :] * u)
    yc = bg_ref[...].astype(jnp.float32) * conv
    yc = yc * _rms_scale(yc, CONV_WIDTH) * gc_ref[...]
    ybuf[:, ATTN_WIDTH:] = yc.astype(jnp.bfloat16)

    o_ref[...] = x_ref[...] + jnp.dot(ybuf[...], wo_ref[...], preferred_element_type=jnp.float32)


def _mixer(x2, q, kv, bg, u, sinks, conv_w, ga, gc, w_out, seq_len):
    t = x2.shape[0]
    tq = TOKEN_TILE
    tiles_per_seq = seq_len // tq
    row = lambda width: pl.BlockSpec((tq, width), lambda i, *_: (i, 0))
    prev_kv = pl.BlockSpec(
        (WINDOW, 4 * LANES), lambda i, *_: (jnp.maximum(i * (tq // WINDOW) - 1, 0), 0))
    prev_u = pl.BlockSpec(
        (SUBLANES, CONV_WIDTH), lambda i, *_: (jnp.maximum(i * (tq // SUBLANES) - 1, 0), 0))
    const = lambda shape: pl.BlockSpec(
        shape, lambda i, *_: (0,) * len(shape), pipeline_mode=pl.Buffered(1))
    grid_spec = pltpu.PrefetchScalarGridSpec(
        num_scalar_prefetch=1,
        grid=(t // tq,),
        in_specs=[
            row(D_MODEL), row(ATTN_WIDTH), row(4 * LANES), prev_kv, row(CONV_WIDTH),
            row(CONV_WIDTH), prev_u,
            const((CONV_K, CONV_WIDTH)), const((1, ATTN_WIDTH)), const((1, CONV_WIDTH)),
            const((D_MODEL, D_MODEL)),
        ],
        out_specs=row(D_MODEL),
        scratch_shapes=[
            pltpu.VMEM((SUBLANES + tq, CONV_WIDTH), jnp.float32),
            pltpu.VMEM((tq, D_MODEL), jnp.bfloat16),
        ],
    )
    return pl.pallas_call(
        functools.partial(_mixer_kernel, tiles_per_seq),
        out_shape=jax.ShapeDtypeStruct((t, D_MODEL), jnp.float32),
        grid_spec=grid_spec,
        compiler_params=pltpu.CompilerParams(
            dimension_semantics=("arbitrary",), vmem_limit_bytes=VMEM_LIMIT_BYTES),
        name="mixer",
    )(sinks, x2, q, kv, kv, bg, u, u, conv_w, ga, gc, w_out)


def _mlp_kernel(x_ref, g_ref, wu_ref, wd_ref, o_ref):
    x = x_ref[...]
    h = (x * _rms_scale(x, D_MODEL) * g_ref[...]).astype(jnp.bfloat16)
    o_ref[...] = x
    for c in range(D_FF // FF_CHUNK):
        c0 = c * FF_CHUNK
        a = jnp.dot(h, wu_ref[:, c0:c0 + FF_CHUNK], preferred_element_type=jnp.float32)
        a = jnp.maximum(a, 0.0)
        a = (a * a).astype(jnp.bfloat16)
        o_ref[...] += jnp.dot(a, wd_ref[c0:c0 + FF_CHUNK, :], preferred_element_type=jnp.float32)


def _mlp(x2, g, w_up, w_down):
    t = x2.shape[0]
    tm = TOKEN_TILE
    row = pl.BlockSpec((tm, D_MODEL), lambda i: (i, 0))
    return pl.pallas_call(
        _mlp_kernel,
        out_shape=jax.ShapeDtypeStruct((t, D_MODEL), jnp.float32),
        grid=(t // tm,),
        in_specs=[row, _const_spec((1, D_MODEL)), _const_spec((D_MODEL, D_FF)),
                  _const_spec((D_FF, D_MODEL))],
        out_specs=row,
        compiler_params=pltpu.CompilerParams(
            dimension_semantics=("arbitrary",), vmem_limit_bytes=VMEM_LIMIT_BYTES),
        name="mlp",
    )(x2, g, w_up, w_down)


def _head_mean_matrix(width):
    head = jnp.arange(width) // HEAD_DIM
    return jnp.where(head[:, None] == head[None, :], 1.0 / HEAD_DIM, 0.0).astype(jnp.bfloat16)


def kernel(x, attn_norm_g, w_in, q_norm_g, k_norm_g, sinks, conv_w, attn_out_g, conv_out_g,
           w_out, mlp_norm_g, w_up, w_down):
    b, s, d = x.shape
    depth = w_in.shape[0]
    assert d == D_MODEL and s % TOKEN_TILE == 0
    bf16 = jnp.bfloat16
    x2 = x.reshape(b * s, d)
    eq = _head_mean_matrix(ATTN_WIDTH)
    ek = _head_mean_matrix(KV_WIDTH)
    for l in range(depth):
        gq = jnp.tile(q_norm_g[l], N_HEADS)[None, :] * (HEAD_DIM ** -0.5)
        gk = jnp.tile(k_norm_g[l], N_KV_HEADS)[None, :]
        q, kv, bg, u = _in_proj(x2, attn_norm_g[l][None, :], w_in[l].astype(bf16), gq, gk, eq, ek)
        x2 = _mixer(x2, q, kv, bg, u, sinks[l], conv_w[l], attn_out_g[l][None, :],
                    conv_out_g[l][None, :], w_out[l].astype(bf16), s)
        x2 = _mlp(x2, mlp_norm_g[l][None, :], w_up[l].astype(bf16), w_down[l].astype(bf16))
    return x2.reshape(b, s, d)
```

```python
import functools
import math

import jax
import jax.numpy as jnp
from jax import lax
from jax.experimental import pallas as pl
from jax.experimental.pallas import tpu as pltpu

D_MODEL = 1024
N_HEADS = 8
HEAD_DIM = 64
N_KV_HEADS = 2
ATTN_WIDTH = N_HEADS * HEAD_DIM
KV_WIDTH = N_KV_HEADS * HEAD_DIM
CONV_WIDTH = D_MODEL - ATTN_WIDTH
CONV_K = 3
WINDOW = 128
IN_COLS = ATTN_WIDTH + 2 * KV_WIDTH + 3 * CONV_WIDTH
D_FF = 4 * D_MODEL
EPS = 1e-6
NEG_INF = -1e30
LOG2E = math.log2(math.e)

LANES = 128
SUBLANES = 8
TOKEN_TILE = 512
FF_CHUNK = 1024
VMEM_LIMIT_BYTES = 56 * 1024 * 1024

GROUP_A_HEADS = (0, 2, 5, 7)
GROUP_B_HEADS = (1, 3, 4, 6)


def _const_spec(shape):
    return pl.BlockSpec(shape, lambda *_: (0,) * len(shape), pipeline_mode=pl.Buffered(1))


def _rms_scale(xf, width):
    return lax.rsqrt(jnp.sum(xf * xf, axis=-1, keepdims=True) * (1.0 / width) + EPS)


def _in_proj_kernel(tiles_per_seq, x_ref, g_ref, w_ref, gq_ref, gk_ref, eq_ref, ek_ref, cw_ref,
                    gc_ref, qa_ref, qb_ref, kv_ref, yc_ref, u_tail):
    tm = TOKEN_TILE
    x = x_ref[...]
    h = (x * _rms_scale(x, D_MODEL) * g_ref[...]).astype(jnp.bfloat16)
    z = jnp.dot(h, w_ref[...], preferred_element_type=jnp.float32)
    o0 = ATTN_WIDTH
    o1 = o0 + KV_WIDTH
    o2 = o1 + KV_WIDTH
    o3 = o2 + CONV_WIDTH
    o4 = o3 + CONV_WIDTH
    q = z[:, :o0]
    k = z[:, o0:o1]
    v = z[:, o1:o2]
    q_ms = jnp.dot((q * q).astype(jnp.bfloat16), eq_ref[...], preferred_element_type=jnp.float32)
    k_ms = jnp.dot((k * k).astype(jnp.bfloat16), ek_ref[...], preferred_element_type=jnp.float32)
    qn = q * lax.rsqrt(q_ms + EPS) * gq_ref[...]
    kn = k * lax.rsqrt(k_ms + EPS) * gk_ref[...]
    col = lax.broadcasted_iota(jnp.int32, (tm, ATTN_WIDTH), 1)
    in_a = ((col % LANES) < HEAD_DIM) == (col < ATTN_WIDTH // 2)
    qa_ref[...] = jnp.where(in_a, qn, 0.0).astype(jnp.bfloat16)
    qb_ref[...] = jnp.where(in_a, 0.0, qn).astype(jnp.bfloat16)
    kv_ref[:, 0 * LANES:1 * LANES] = kn.astype(jnp.bfloat16)
    kv_ref[:, 1 * LANES:2 * LANES] = pltpu.roll(kn, HEAD_DIM, 1).astype(jnp.bfloat16)
    kv_ref[:, 2 * LANES:3 * LANES] = v.astype(jnp.bfloat16)
    kv_ref[:, 3 * LANES:4 * LANES] = pltpu.roll(v, HEAD_DIM, 1).astype(jnp.bfloat16)

    u = z[:, o3:o4] * z[:, o4:]
    first = (pl.program_id(0) % tiles_per_seq) == 0
    prev = jnp.where(first, 0.0, u_tail[...])
    ext = jnp.concatenate([prev, u], axis=0)
    u1 = pltpu.roll(ext, 1, 0)[SUBLANES:]
    u2 = pltpu.roll(ext, 2, 0)[SUBLANES:]
    u_tail[...] = u[tm - SUBLANES:]
    conv = cw_ref[0:1, :] * u2 + cw_ref[1:2, :] * u1 + cw_ref[2:3, :] * u
    yc = z[:, o2:o3] * conv
    yc_ref[...] = (yc * _rms_scale(yc, CONV_WIDTH) * gc_ref[...]).astype(jnp.bfloat16)


def _in_proj(x2, g, w_in, gq, gk, eq, ek, conv_w, gc, seq_len):
    t = x2.shape[0]
    tm = TOKEN_TILE
    row = lambda width: pl.BlockSpec((tm, width), lambda i: (i, 0))
    out_shape = (
        jax.ShapeDtypeStruct((t, ATTN_WIDTH), jnp.bfloat16),
        jax.ShapeDtypeStruct((t, ATTN_WIDTH), jnp.bfloat16),
        jax.ShapeDtypeStruct((t, 4 * LANES), jnp.bfloat16),
        jax.ShapeDtypeStruct((t, CONV_WIDTH), jnp.bfloat16),
    )
    return pl.pallas_call(
        functools.partial(_in_proj_kernel, seq_len // tm),
        out_shape=out_shape,
        grid=(t // tm,),
        in_specs=[
            row(D_MODEL),
            _const_spec((1, D_MODEL)),
            _const_spec((D_MODEL, IN_COLS)),
            _const_spec((1, ATTN_WIDTH)),
            _const_spec((1, KV_WIDTH)),
            _const_spec((ATTN_WIDTH, ATTN_WIDTH)),
            _const_spec((KV_WIDTH, KV_WIDTH)),
            _const_spec((CONV_K, CONV_WIDTH)),
            _const_spec((1, CONV_WIDTH)),
        ],
        out_specs=(row(ATTN_WIDTH), row(ATTN_WIDTH), row(4 * LANES), row(CONV_WIDTH)),
        scratch_shapes=[pltpu.VMEM((SUBLANES, CONV_WIDTH), jnp.float32)],
        compiler_params=pltpu.CompilerParams(
            dimension_semantics=("arbitrary",), vmem_limit_bytes=VMEM_LIMIT_BYTES),
        name="in_proj",
    )(x2, g, w_in, gq, gk, eq, ek, conv_w, gc)


def _mixer_kernel(tiles_per_seq, sinks_ref, x_ref, qa_ref, qb_ref, kv_ref, kvp_ref, yc_ref,
                  ga_ref, wo_ref, o_ref, ybuf):
    tq = TOKEN_TILE
    first = (pl.program_id(0) % tiles_per_seq) == 0

    lane = lax.broadcasted_iota(jnp.int32, (WINDOW, LANES), 1)
    lo = lane < HEAD_DIM
    qi = lax.broadcasted_iota(jnp.int32, (WINDOW, 2 * WINDOW), 0)
    kj = lax.broadcasted_iota(jnp.int32, (WINDOW, 2 * WINDOW), 1)
    band = (kj > qi) & (kj <= qi + WINDOW)
    band_first = band & (kj >= jnp.where(first, WINDOW, 0))
    band4 = jnp.concatenate([band] * 4, axis=0)
    band4_first = jnp.concatenate([band_first] * 4, axis=0)

    def sink_rows(heads):
        return jnp.concatenate(
            [jnp.full((WINDOW, LANES), sinks_ref[h], jnp.float32) for h in heads], axis=0)

    sink_a = sink_rows(GROUP_A_HEADS)
    sink_b = sink_rows(GROUP_B_HEADS)
    ones = jnp.ones((2 * WINDOW, LANES), jnp.bfloat16)

    def attend(q_ref, r0, k, v, sink, mask):
        qs = jnp.concatenate(
            [q_ref[r0:r0 + WINDOW, s * LANES:(s + 1) * LANES] for s in range(4)], axis=0)
        s = lax.dot_general(qs, k, (((1,), (1,)), ((), ())),
                            preferred_element_type=jnp.float32)
        s = jnp.where(mask, s, NEG_INF)
        m = jnp.max(jnp.maximum(s[:, :LANES], s[:, LANES:]), axis=-1, keepdims=True)
        m = jnp.maximum(jnp.broadcast_to(m, sink.shape), sink)
        p = jnp.exp2(s - jnp.concatenate([m, m], axis=1))
        ov = jnp.dot(p.astype(jnp.bfloat16), jnp.concatenate([v, ones], axis=1),
                     preferred_element_type=jnp.float32)
        l = ov[:, LANES:] + jnp.exp2(sink - m)
        return ov[:, :LANES] / l

    for j in range(tq // WINDOW):
        r0 = j * WINDOW
        if j == 0:
            kv = jnp.concatenate([kvp_ref[...], kv_ref[0:WINDOW, :]], axis=0)
        else:
            kv = kv_ref[r0 - WINDOW:r0 + WINDOW, :]
        mask = band4_first if j == 0 else band4
        o_a = attend(qa_ref, r0, kv[:, 0:LANES], kv[:, 2 * LANES:3 * LANES], sink_a, mask)
        o_b = attend(qb_ref, r0, kv[:, LANES:2 * LANES], kv[:, 3 * LANES:4 * LANES], sink_b, mask)
        rows = lambda o, s: o[s * WINDOW:(s + 1) * WINDOW]
        y = jnp.concatenate([jnp.where(lo, rows(o_a, 0), rows(o_b, 0)),
                             jnp.where(lo, rows(o_a, 1), rows(o_b, 1)),
                             jnp.where(lo, rows(o_b, 2), rows(o_a, 2)),
                             jnp.where(lo, rows(o_b, 3), rows(o_a, 3))], axis=1)
        y = y * _rms_scale(y, ATTN_WIDTH) * ga_ref[...]
        ybuf[r0:r0 + WINDOW, 0:ATTN_WIDTH] = y.astype(jnp.bfloat16)

    ybuf[:, ATTN_WIDTH:] = yc_ref[...]
    o_ref[...] = x_ref[...] + jnp.dot(ybuf[...], wo_ref[...], preferred_element_type=jnp.float32)


def _mixer(x2, qa, qb, kv, yc, sinks, ga, w_out, seq_len):
    t = x2.shape[0]
    tq = TOKEN_TILE
    row = lambda width: pl.BlockSpec((tq, width), lambda i, *_: (i, 0))
    prev_kv = pl.BlockSpec(
        (WINDOW, 4 * LANES), lambda i, *_: (jnp.maximum(i * (tq // WINDOW) - 1, 0), 0))
    const = lambda shape: pl.BlockSpec(
        shape, lambda i, *_: (0,) * len(shape), pipeline_mode=pl.Buffered(1))
    grid_spec = pltpu.PrefetchScalarGridSpec(
        num_scalar_prefetch=1,
        grid=(t // tq,),
        in_specs=[
            row(D_MODEL), row(ATTN_WIDTH), row(ATTN_WIDTH), row(4 * LANES), prev_kv,
            row(CONV_WIDTH), const((1, ATTN_WIDTH)), const((D_MODEL, D_MODEL)),
        ],
        out_specs=row(D_MODEL),
        scratch_shapes=[pltpu.VMEM((tq, D_MODEL), jnp.bfloat16)],
    )
    return pl.pallas_call(
        functools.partial(_mixer_kernel, seq_len // tq),
        out_shape=jax.ShapeDtypeStruct((t, D_MODEL), jnp.float32),
        grid_spec=grid_spec,
        compiler_params=pltpu.CompilerParams(
            dimension_semantics=("arbitrary",), vmem_limit_bytes=VMEM_LIMIT_BYTES),
        name="mixer",
    )(sinks, x2, qa, qb, kv, kv, yc, ga, w_out)


def _mlp_kernel(x_ref, g_ref, wu_ref, wd_ref, o_ref):
    x = x_ref[...]
    h = (x * _rms_scale(x, D_MODEL) * g_ref[...]).astype(jnp.bfloat16)
    o_ref[...] = x
    for c in range(D_FF // FF_CHUNK):
        c0 = c * FF_CHUNK
        a = jnp.dot(h, wu_ref[:, c0:c0 + FF_CHUNK], preferred_element_type=jnp.float32)
        a = jnp.maximum(a, 0.0)
        a = (a * a).astype(jnp.bfloat16)
        o_ref[...] += jnp.dot(a, wd_ref[c0:c0 + FF_CHUNK, :], preferred_element_type=jnp.float32)


def _mlp(x2, g, w_up, w_down):
    t = x2.shape[0]
    tm = TOKEN_TILE
    row = pl.BlockSpec((tm, D_MODEL), lambda i: (i, 0))
    return pl.pallas_call(
        _mlp_kernel,
        out_shape=jax.ShapeDtypeStruct((t, D_MODEL), jnp.float32),
        grid=(t // tm,),
        in_specs=[row, _const_spec((1, D_MODEL)), _const_spec((D_MODEL, D_FF)),
                  _const_spec((D_FF, D_MODEL))],
        out_specs=row,
        compiler_params=pltpu.CompilerParams(
            dimension_semantics=("arbitrary",), vmem_limit_bytes=VMEM_LIMIT_BYTES),
        name="mlp",
    )(x2, g, w_up, w_down)


def _head_mean_matrix(width):
    head = jnp.arange(width) // HEAD_DIM
    return jnp.where(head[:, None] == head[None, :], 1.0 / HEAD_DIM, 0.0).astype(jnp.bfloat16)


def kernel(x, attn_norm_g, w_in, q_norm_g, k_norm_g, sinks, conv_w, attn_out_g, conv_out_g,
           w_out, mlp_norm_g, w_up, w_down):
    b, s, d = x.shape
    depth = w_in.shape[0]
    assert d == D_MODEL and s % TOKEN_TILE == 0
    bf16 = jnp.bfloat16
    x2 = x.reshape(b * s, d)
    eq = _head_mean_matrix(ATTN_WIDTH)
    ek = _head_mean_matrix(KV_WIDTH)
    for l in range(depth):
        gq = jnp.tile(q_norm_g[l], N_HEADS)[None, :] * (HEAD_DIM ** -0.5 * LOG2E)
        gk = jnp.tile(k_norm_g[l], N_KV_HEADS)[None, :]
        qa, qb, kv, yc = _in_proj(x2, attn_norm_g[l][None, :], w_in[l].astype(bf16), gq, gk, eq, ek,
                                  conv_w[l], conv_out_g[l][None, :], s)
        x2 = _mixer(x2, qa, qb, kv, yc, sinks[l] * LOG2E, attn_out_g[l][None, :],
                    w_out[l].astype(bf16), s)
        x2 = _mlp(x2, mlp_norm_g[l][None, :], w_up[l].astype(bf16), w_down[l].astype(bf16))
    return x2.reshape(b, s, d)
```

```python
import functools
import math

import jax
import jax.numpy as jnp
from jax import lax
from jax.experimental import pallas as pl
from jax.experimental.pallas import tpu as pltpu

D_MODEL = 1024
N_HEADS = 8
HEAD_DIM = 64
N_KV_HEADS = 2
ATTN_WIDTH = N_HEADS * HEAD_DIM
KV_WIDTH = N_KV_HEADS * HEAD_DIM
CONV_WIDTH = D_MODEL - ATTN_WIDTH
CONV_K = 3
WINDOW = 128
IN_COLS = ATTN_WIDTH + 2 * KV_WIDTH + 3 * CONV_WIDTH
D_FF = 4 * D_MODEL
EPS = 1e-6
NEG_INF = -1e30
LOG2E = math.log2(math.e)

LANES = 128
SUBLANES = 8
TOKEN_TILE = 512
FF_CHUNK = 1024
VMEM_LIMIT_BYTES = 56 * 1024 * 1024

GROUP_A_HEADS = (0, 2, 5, 7)
GROUP_B_HEADS = (1, 3, 4, 6)


def _rms_scale(xf, width):
    return lax.rsqrt(jnp.sum(xf * xf, axis=-1, keepdims=True) * (1.0 / width) + EPS)


def _dot(a, b):
    return jnp.dot(a, b, preferred_element_type=jnp.float32)


def _layer_kernel(tiles_per_seq, n_tiles, sinks_ref, x_ref, g1_ref, win_ref, gq_ref, gk_ref,
                  eq_ref, ek_ref, cw_ref, gc_ref, ga_ref, wo_ref, g2_ref, wu_ref, wd_ref,
                  o_ref, x1_prev, kv_tail, u_tail, ybuf):
    tm = TOKEN_TILE
    step = pl.program_id(0)

    @pl.when(step == 0)
    def _():
        x1_prev[...] = jnp.zeros_like(x1_prev)
        kv_tail[...] = jnp.zeros_like(kv_tail)
        u_tail[...] = jnp.zeros_like(u_tail)

    xp = x1_prev[...]
    hp = (xp * _rms_scale(xp, D_MODEL) * g2_ref[...]).astype(jnp.bfloat16)
    o_ref[...] = xp
    for c in range(D_FF // FF_CHUNK):
        c0 = c * FF_CHUNK
        a = jnp.maximum(_dot(hp, wu_ref[:, c0:c0 + FF_CHUNK]), 0.0)
        o_ref[...] += _dot((a * a).astype(jnp.bfloat16), wd_ref[c0:c0 + FF_CHUNK, :])

    tile = jnp.minimum(step, n_tiles - 1)
    first = (tile % tiles_per_seq) == 0
    x = x_ref[...]
    h = (x * _rms_scale(x, D_MODEL) * g1_ref[...]).astype(jnp.bfloat16)
    o0 = ATTN_WIDTH
    o1 = o0 + 2 * KV_WIDTH
    zq = _dot(h, win_ref[:, :o0])
    zkv = _dot(h, win_ref[:, o0:o1])
    zc = _dot(h, win_ref[:, o1:])
    k = zkv[:, :KV_WIDTH]
    v = zkv[:, KV_WIDTH:]
    half = ATTN_WIDTH // 2
    q2 = (zq * zq).astype(jnp.bfloat16)
    q_ms = jnp.concatenate([_dot(q2[:, :half], eq_ref[...]), _dot(q2[:, half:], eq_ref[...])], axis=1)
    k_ms = _dot((k * k).astype(jnp.bfloat16), ek_ref[...])
    qn = zq * lax.rsqrt(q_ms + EPS) * gq_ref[...]
    kn = k * lax.rsqrt(k_ms + EPS) * gk_ref[...]
    col = lax.broadcasted_iota(jnp.int32, (tm, ATTN_WIDTH), 1)
    in_a = ((col % LANES) < HEAD_DIM) == (col < half)
    qa = jnp.where(in_a, qn, 0.0).astype(jnp.bfloat16)
    qb = jnp.where(in_a, 0.0, qn).astype(jnp.bfloat16)
    bf = lambda t: t.astype(jnp.bfloat16)
    kv = jnp.concatenate([bf(kn), bf(pltpu.roll(kn, HEAD_DIM, 1)),
                          bf(v), bf(pltpu.roll(v, HEAD_DIM, 1))], axis=1)
    kv_prev = kv_tail[...]
    kv_tail[...] = kv[tm - WINDOW:]

    u = zc[:, CONV_WIDTH:2 * CONV_WIDTH] * zc[:, 2 * CONV_WIDTH:]
    ext = jnp.concatenate([jnp.where(first, 0.0, u_tail[...]), u], axis=0)
    u1 = pltpu.roll(ext, 1, 0)[SUBLANES:]
    u2 = pltpu.roll(ext, 2, 0)[SUBLANES:]
    u_tail[...] = u[tm - SUBLANES:]
    conv = cw_ref[0:1, :] * u2 + cw_ref[1:2, :] * u1 + cw_ref[2:3, :] * u
    yc = zc[:, :CONV_WIDTH] * conv
    ybuf[:, ATTN_WIDTH:] = (yc * _rms_scale(yc, CONV_WIDTH) * gc_ref[...]).astype(jnp.bfloat16)

    lane = lax.broadcasted_iota(jnp.int32, (WINDOW, LANES), 1)
    lo = lane < HEAD_DIM
    qi = lax.broadcasted_iota(jnp.int32, (WINDOW, 2 * WINDOW), 0)
    kj = lax.broadcasted_iota(jnp.int32, (WINDOW, 2 * WINDOW), 1)
    band = (kj > qi) & (kj <= qi + WINDOW)
    band_first = band & (kj >= jnp.where(first, WINDOW, 0))
    band4 = jnp.concatenate([band] * 4, axis=0)
    band4_first = jnp.concatenate([band_first] * 4, axis=0)

    def sink_rows(heads):
        return jnp.concatenate(
            [jnp.full((WINDOW, LANES), sinks_ref[hd], jnp.float32) for hd in heads], axis=0)

    sink_a = sink_rows(GROUP_A_HEADS)
    sink_b = sink_rows(GROUP_B_HEADS)
    ones = jnp.ones((2 * WINDOW, LANES), jnp.bfloat16)

    def attend(qg, r0, kk, vv, sink, mask):
        qs = jnp.concatenate(
            [qg[r0:r0 + WINDOW, sl * LANES:(sl + 1) * LANES] for sl in range(4)], axis=0)
        s = lax.dot_general(qs, kk, (((1,), (1,)), ((), ())),
                            preferred_element_type=jnp.float32)
        s = jnp.where(mask, s, NEG_INF)
        m = jnp.max(jnp.maximum(s[:, :LANES], s[:, LANES:]), axis=-1, keepdims=True)
        m = jnp.maximum(jnp.broadcast_to(m, sink.shape), sink)
        p = jnp.exp2(s - jnp.concatenate([m, m], axis=1))
        ov = _dot(p.astype(jnp.bfloat16), jnp.concatenate([vv, ones], axis=1))
        l = ov[:, LANES:] + jnp.exp2(sink - m)
        return ov[:, :LANES] / l

    for j in range(tm // WINDOW):
        r0 = j * WINDOW
        if j == 0:
            kvw = jnp.concatenate([kv_prev, kv[0:WINDOW]], axis=0)
        else:
            kvw = kv[r0 - WINDOW:r0 + WINDOW]
        mask = band4_first if j == 0 else band4
        o_a = attend(qa, r0, kvw[:, 0:LANES], kvw[:, 2 * LANES:3 * LANES], sink_a, mask)
        o_b = attend(qb, r0, kvw[:, LANES:2 * LANES], kvw[:, 3 * LANES:4 * LANES], sink_b, mask)
        rows = lambda o, sl: o[sl * WINDOW:(sl + 1) * WINDOW]
        y = jnp.concatenate([jnp.where(lo, rows(o_a, 0), rows(o_b, 0)),
                             jnp.where(lo, rows(o_a, 1), rows(o_b, 1)),
                             jnp.where(lo, rows(o_b, 2), rows(o_a, 2)),
                             jnp.where(lo, rows(o_b, 3), rows(o_a, 3))], axis=1)
        y = y * _rms_scale(y, ATTN_WIDTH) * ga_ref[...]
        ybuf[r0:r0 + WINDOW, 0:ATTN_WIDTH] = y.astype(jnp.bfloat16)

    x1_prev[...] = x + _dot(ybuf[...], wo_ref[...])


def _layer(x2, sinks, g1, w_in, gq, gk, eq, ek, conv_w, gc, ga, w_out, g2, w_up, w_down, seq_len):
    t = x2.shape[0]
    tm = TOKEN_TILE
    n_tiles = t // tm
    const = lambda shape: pl.BlockSpec(
        shape, lambda i, *_: (0,) * len(shape), pipeline_mode=pl.Buffered(1))
    grid_spec = pltpu.PrefetchScalarGridSpec(
        num_scalar_prefetch=1,
        grid=(n_tiles + 1,),
        in_specs=[
            pl.BlockSpec((tm, D_MODEL), lambda i, *_: (jnp.minimum(i, n_tiles - 1), 0)),
            const((1, D_MODEL)), const((D_MODEL, IN_COLS)), const((1, ATTN_WIDTH)),
            const((1, KV_WIDTH)), const((ATTN_WIDTH // 2, ATTN_WIDTH // 2)),
            const((KV_WIDTH, KV_WIDTH)), const((CONV_K, CONV_WIDTH)), const((1, CONV_WIDTH)),
            const((1, ATTN_WIDTH)), const((D_MODEL, D_MODEL)), const((1, D_MODEL)),
            const((D_MODEL, D_FF)), const((D_FF, D_MODEL)),
        ],
        out_specs=pl.BlockSpec((tm, D_MODEL), lambda i, *_: (jnp.maximum(i - 1, 0), 0)),
        scratch_shapes=[
            pltpu.VMEM((tm, D_MODEL), jnp.float32),
            pltpu.VMEM((WINDOW, 4 * LANES), jnp.bfloat16),
            pltpu.VMEM((SUBLANES, CONV_WIDTH), jnp.float32),
            pltpu.VMEM((tm, D_MODEL), jnp.bfloat16),
        ],
    )
    return pl.pallas_call(
        functools.partial(_layer_kernel, seq_len // tm, n_tiles),
        out_shape=jax.ShapeDtypeStruct((t, D_MODEL), jnp.float32),
        grid_spec=grid_spec,
        compiler_params=pltpu.CompilerParams(
            dimension_semantics=("arbitrary",), vmem_limit_bytes=VMEM_LIMIT_BYTES),
        name="layer",
    )(sinks, x2, g1, w_in, gq, gk, eq, ek, conv_w, gc, ga, w_out, g2, w_up, w_down)


def _head_mean_matrix(width):
    head = jnp.arange(width) // HEAD_DIM
    return jnp.where(head[:, None] == head[None, :], 1.0 / HEAD_DIM, 0.0).astype(jnp.bfloat16)


def kernel(x, attn_norm_g, w_in, q_norm_g, k_norm_g, sinks, conv_w, attn_out_g, conv_out_g,
           w_out, mlp_norm_g, w_up, w_down):
    b, s, d = x.shape
    depth = w_in.shape[0]
    assert d == D_MODEL and s % TOKEN_TILE == 0
    bf16 = jnp.bfloat16
    x2 = x.reshape(b * s, d)
    eq = _head_mean_matrix(ATTN_WIDTH // 2)
    ek = _head_mean_matrix(KV_WIDTH)
    for l in range(depth):
        gq = jnp.tile(q_norm_g[l], N_HEADS)[None, :] * (HEAD_DIM ** -0.5 * LOG2E)
        gk = jnp.tile(k_norm_g[l], N_KV_HEADS)[None, :]
        x2 = _layer(x2, sinks[l] * LOG2E, attn_norm_g[l][None, :], w_in[l].astype(bf16), gq, gk,
                    eq, ek, conv_w[l], conv_out_g[l][None, :], attn_out_g[l][None, :],
                    w_out[l].astype(bf16), mlp_norm_g[l][None, :], w_up[l].astype(bf16),
                    w_down[l].astype(bf16), s)
    return x2.reshape(b, s, d)
```

```python
import functools
import math

import jax
import jax.numpy as jnp
from jax import lax
from jax.experimental import pallas as pl
from jax.experimental.pallas import tpu as pltpu

D_MODEL = 1024
N_HEADS = 8
HEAD_DIM = 64
N_KV_HEADS = 2
ATTN_WIDTH = N_HEADS * HEAD_DIM
KV_WIDTH = N_KV_HEADS * HEAD_DIM
CONV_WIDTH = D_MODEL - ATTN_WIDTH
CONV_K = 3
WINDOW = 128
IN_COLS = ATTN_WIDTH + 2 * KV_WIDTH + 3 * CONV_WIDTH
D_FF = 4 * D_MODEL
EPS = 1e-6
NEG_INF = -1e30
LOG2E = math.log2(math.e)

LANES = 128
SUBLANES = 8
TOKEN_TILE = 512
FF_CHUNK = 1024
MXU_COLS = 256
MLP_PIECES_BEFORE_ATTENTION = 2
MLP_PIECES_PER_BLOCK = 2
VMEM_LIMIT_BYTES = 56 * 1024 * 1024

GROUP_A_HEADS = (0, 2, 5, 7)
GROUP_B_HEADS = (1, 3, 4, 6)


def _rms_scale(xf, width):
    return lax.rsqrt(jnp.sum(xf * xf, axis=-1, keepdims=True) * (1.0 / width) + EPS)


def _dot(a, b):
    return jnp.dot(a, b, preferred_element_type=jnp.float32)


def _layer_kernel(tiles_per_seq, n_tiles, sinks_ref, x_ref, g1_ref, win_ref, gq_ref, gk_ref,
                  eq_ref, ek_ref, cw_ref, gc_ref, ga_ref, wo_ref, g2_ref, wu_ref, wd_ref,
                  o_ref, x1_prev, h_prev, kv_tail, u_tail, ybuf):
    tm = TOKEN_TILE
    step = pl.program_id(0)

    @pl.when(step == 0)
    def _():
        x1_prev[...] = jnp.zeros_like(x1_prev)
        h_prev[...] = jnp.zeros_like(h_prev)
        kv_tail[...] = jnp.zeros_like(kv_tail)
        u_tail[...] = jnp.zeros_like(u_tail)

    o_ref[...] = x1_prev[...]
    n_blk = FF_CHUNK // MXU_COLS
    acts = {}

    def up_piece(c, n):
        w0 = c * FF_CHUNK + n * MXU_COLS
        a = jnp.maximum(_dot(h_prev[...], wu_ref[:, w0:w0 + MXU_COLS]), 0.0)
        acts[c, n] = (a * a).astype(jnp.bfloat16)

    def down_piece(c, n):
        a = jnp.concatenate([acts[c, i] for i in range(n_blk)], axis=1)
        o_ref[:, n * MXU_COLS:(n + 1) * MXU_COLS] += _dot(
            a, wd_ref[c * FF_CHUNK:(c + 1) * FF_CHUNK, n * MXU_COLS:(n + 1) * MXU_COLS])

    pieces = [functools.partial(f, c, n) for c in range(D_FF // FF_CHUNK)
              for f in (up_piece, down_piece) for n in range(n_blk)]

    def mlp_pieces(count):
        for _ in range(min(count, len(pieces))):
            pieces.pop(0)()

    mlp_pieces(4)

    tile = jnp.minimum(step, n_tiles - 1)
    first = (tile % tiles_per_seq) == 0
    x = x_ref[...]
    h = (x * _rms_scale(x, D_MODEL) * g1_ref[...]).astype(jnp.bfloat16)
    o0 = ATTN_WIDTH
    o1 = o0 + 2 * KV_WIDTH
    zq = _dot(h, win_ref[:, :o0])
    zkv = _dot(h, win_ref[:, o0:o1])
    k = zkv[:, :KV_WIDTH]
    v = zkv[:, KV_WIDTH:]
    half = ATTN_WIDTH // 2
    q2 = (zq * zq).astype(jnp.bfloat16)
    q_ms = jnp.concatenate([_dot(q2[:, :half], eq_ref[...]), _dot(q2[:, half:], eq_ref[...])], axis=1)
    k_ms = _dot((k * k).astype(jnp.bfloat16), ek_ref[...])
    zc = _dot(h, win_ref[:, o1:])
    qn = zq * lax.rsqrt(q_ms + EPS) * gq_ref[...]
    kn = k * lax.rsqrt(k_ms + EPS) * gk_ref[...]
    col = lax.broadcasted_iota(jnp.int32, (tm, ATTN_WIDTH), 1)
    in_a = ((col % LANES) < HEAD_DIM) == (col < half)
    qa = jnp.where(in_a, qn, 0.0).astype(jnp.bfloat16)
    qb = jnp.where(in_a, 0.0, qn).astype(jnp.bfloat16)
    bf = lambda t: t.astype(jnp.bfloat16)
    kv = jnp.concatenate([bf(kn), bf(pltpu.roll(kn, HEAD_DIM, 1)),
                          bf(v), bf(pltpu.roll(v, HEAD_DIM, 1))], axis=1)
    kv_prev = kv_tail[...]
    kv_tail[...] = kv[tm - WINDOW:]

    u = zc[:, CONV_WIDTH:2 * CONV_WIDTH] * zc[:, 2 * CONV_WIDTH:]
    ext = jnp.concatenate([jnp.where(first, 0.0, u_tail[...]), u], axis=0)
    u1 = pltpu.roll(ext, 1, 0)[SUBLANES:]
    u2 = pltpu.roll(ext, 2, 0)[SUBLANES:]
    u_tail[...] = u[tm - SUBLANES:]
    conv = cw_ref[0:1, :] * u2 + cw_ref[1:2, :] * u1 + cw_ref[2:3, :] * u
    yc = zc[:, :CONV_WIDTH] * conv
    ybuf[:, ATTN_WIDTH:] = (yc * _rms_scale(yc, CONV_WIDTH) * gc_ref[...]).astype(jnp.bfloat16)

    lane = lax.broadcasted_iota(jnp.int32, (WINDOW, LANES), 1)
    lo = lane < HEAD_DIM
    qi = lax.broadcasted_iota(jnp.int32, (WINDOW, 2 * WINDOW), 0)
    kj = lax.broadcasted_iota(jnp.int32, (WINDOW, 2 * WINDOW), 1)
    band = (kj > qi) & (kj <= qi + WINDOW)
    band_first = band & (kj >= jnp.where(first, WINDOW, 0))
    band4 = jnp.concatenate([band] * 4, axis=0)
    band4_first = jnp.concatenate([band_first] * 4, axis=0)

    def sink_rows(heads):
        return jnp.concatenate(
            [jnp.full((WINDOW, LANES), sinks_ref[hd], jnp.float32) for hd in heads], axis=0)

    sink_a = sink_rows(GROUP_A_HEADS)
    sink_b = sink_rows(GROUP_B_HEADS)
    ones = jnp.ones((2 * WINDOW, LANES), jnp.bfloat16)

    def scores(qg, r0, kk, sink, mask):
        qs = jnp.concatenate(
            [qg[r0:r0 + WINDOW, sl * LANES:(sl + 1) * LANES] for sl in range(4)], axis=0)
        s = lax.dot_general(qs, kk, (((1,), (1,)), ((), ())),
                            preferred_element_type=jnp.float32)
        s = jnp.where(mask, s, NEG_INF)
        m = jnp.max(jnp.maximum(s[:, :LANES], s[:, LANES:]), axis=-1, keepdims=True)
        m = jnp.maximum(jnp.broadcast_to(m, sink.shape), sink)
        p = jnp.exp2(s - jnp.concatenate([m, m], axis=1))
        return p.astype(jnp.bfloat16), m

    def weighted(p, m, vv, sink):
        ov = _dot(p, jnp.concatenate([vv, ones], axis=1))
        l = ov[:, LANES:] + jnp.exp2(sink - m)
        return ov[:, :LANES] / l

    def kv_window(j):
        if j == 0:
            return jnp.concatenate([kv_prev, kv[0:WINDOW]], axis=0)
        return kv[(j - 1) * WINDOW:(j + 1) * WINDOW]

    def block_scores(j):
        kvw = kv_window(j)
        mask = band4_first if j == 0 else band4
        return (scores(qa, j * WINDOW, kvw[:, 0:LANES], sink_a, mask),
                scores(qb, j * WINDOW, kvw[:, LANES:2 * LANES], sink_b, mask))

    def block_output(j, pm):
        kvw = kv_window(j)
        (p_a, m_a), (p_b, m_b) = pm
        o_a = weighted(p_a, m_a, kvw[:, 2 * LANES:3 * LANES], sink_a)
        o_b = weighted(p_b, m_b, kvw[:, 3 * LANES:4 * LANES], sink_b)
        rows = lambda o, sl: o[sl * WINDOW:(sl + 1) * WINDOW]
        y = jnp.concatenate([jnp.where(lo, rows(o_a, 0), rows(o_b, 0)),
                             jnp.where(lo, rows(o_a, 1), rows(o_b, 1)),
                             jnp.where(lo, rows(o_b, 2), rows(o_a, 2)),
                             jnp.where(lo, rows(o_b, 3), rows(o_a, 3))], axis=1)
        y = y * _rms_scale(y, ATTN_WIDTH) * ga_ref[...]
        ybuf[j * WINDOW:(j + 1) * WINDOW, 0:ATTN_WIDTH] = y.astype(jnp.bfloat16)

    n_blocks = tm // WINDOW
    mlp_pieces(MLP_PIECES_BEFORE_ATTENTION)
    pending = block_scores(0)
    for j in range(n_blocks):
        upcoming = block_scores(j + 1) if j + 1 < n_blocks else None
        mlp_pieces(MLP_PIECES_PER_BLOCK)
        block_output(j, pending)
        pending = upcoming
    mlp_pieces(MLP_PIECES_PER_BLOCK)

    x1 = x + _dot(ybuf[...], wo_ref[...])
    mlp_pieces(len(pieces))

    x1_prev[...] = x1
    h_prev[...] = (x1 * _rms_scale(x1, D_MODEL) * g2_ref[...]).astype(jnp.bfloat16)


def _layer(x2, sinks, g1, w_in, gq, gk, eq, ek, conv_w, gc, ga, w_out, g2, w_up, w_down, seq_len):
    t = x2.shape[0]
    tm = TOKEN_TILE
    n_tiles = t // tm
    const = lambda shape: pl.BlockSpec(
        shape, lambda i, *_: (0,) * len(shape), pipeline_mode=pl.Buffered(1))
    grid_spec = pltpu.PrefetchScalarGridSpec(
        num_scalar_prefetch=1,
        grid=(n_tiles + 1,),
        in_specs=[
            pl.BlockSpec((tm, D_MODEL), lambda i, *_: (jnp.minimum(i, n_tiles - 1), 0)),
            const((1, D_MODEL)), const((D_MODEL, IN_COLS)), const((1, ATTN_WIDTH)),
            const((1, KV_WIDTH)), const((ATTN_WIDTH // 2, ATTN_WIDTH // 2)),
            const((KV_WIDTH, KV_WIDTH)), const((CONV_K, CONV_WIDTH)), const((1, CONV_WIDTH)),
            const((1, ATTN_WIDTH)), const((D_MODEL, D_MODEL)), const((1, D_MODEL)),
            const((D_MODEL, D_FF)), const((D_FF, D_MODEL)),
        ],
        out_specs=pl.BlockSpec((tm, D_MODEL), lambda i, *_: (jnp.maximum(i - 1, 0), 0)),
        scratch_shapes=[
            pltpu.VMEM((tm, D_MODEL), jnp.float32),
            pltpu.VMEM((tm, D_MODEL), jnp.bfloat16),
            pltpu.VMEM((WINDOW, 4 * LANES), jnp.bfloat16),
            pltpu.VMEM((SUBLANES, CONV_WIDTH), jnp.float32),
            pltpu.VMEM((tm, D_MODEL), jnp.bfloat16),
        ],
    )
    return pl.pallas_call(
        functools.partial(_layer_kernel, seq_len // tm, n_tiles),
        out_shape=jax.ShapeDtypeStruct((t, D_MODEL), jnp.float32),
        grid_spec=grid_spec,
        compiler_params=pltpu.CompilerParams(
            dimension_semantics=("arbitrary",), vmem_limit_bytes=VMEM_LIMIT_BYTES),
        name="layer",
    )(sinks, x2, g1, w_in, gq, gk, eq, ek, conv_w, gc, ga, w_out, g2, w_up, w_down)


def _head_mean_matrix(width):
    head = jnp.arange(width) // HEAD_DIM
    return jnp.where(head[:, None] == head[None, :], 1.0 / HEAD_DIM, 0.0).astype(jnp.bfloat16)


def kernel(x, attn_norm_g, w_in, q_norm_g, k_norm_g, sinks, conv_w, attn_out_g, conv_out_g,
           w_out, mlp_norm_g, w_up, w_down):
    b, s, d = x.shape
    depth = w_in.shape[0]
    assert d == D_MODEL and s % TOKEN_TILE == 0
    bf16 = jnp.bfloat16
    x2 = x.reshape(b * s, d)
    eq = _head_mean_matrix(ATTN_WIDTH // 2)
    ek = _head_mean_matrix(KV_WIDTH)
    for l in range(depth):
        gq = jnp.tile(q_norm_g[l], N_HEADS)[None, :] * (HEAD_DIM ** -0.5 * LOG2E)
        gk = jnp.tile(k_norm_g[l], N_KV_HEADS)[None, :]
        x2 = _layer(x2, sinks[l] * LOG2E, attn_norm_g[l][None, :], w_in[l].astype(bf16), gq, gk,
                    eq, ek, conv_w[l], conv_out_g[l][None, :], attn_out_g[l][None, :],
                    w_out[l].astype(bf16), mlp_norm_g[l][None, :], w_up[l].astype(bf16),
                    w_down[l].astype(bf16), s)
    return x2.reshape(b, s, d)
```

```python
import functools
import math

import jax
import jax.numpy as jnp
from jax import lax
from jax.experimental import pallas as pl
from jax.experimental.pallas import tpu as pltpu

D_MODEL = 1024
N_HEADS = 8
HEAD_DIM = 64
N_KV_HEADS = 2
ATTN_WIDTH = N_HEADS * HEAD_DIM
KV_WIDTH = N_KV_HEADS * HEAD_DIM
CONV_WIDTH = D_MODEL - ATTN_WIDTH
CONV_K = 3
WINDOW = 128
IN_COLS = ATTN_WIDTH + 2 * KV_WIDTH + 3 * CONV_WIDTH
D_FF = 4 * D_MODEL
EPS = 1e-6
NEG_INF = -1e30
LOG2E = math.log2(math.e)

LANES = 128
SUBLANES = 8
TOKEN_TILE = 512
FF_CHUNK = 1024
MXU_COLS = 256
MLP_PIECES_BEFORE_ATTENTION = 2
MLP_PIECES_PER_BLOCK = 2
MLP_PIECES_BEFORE_OUT_PROJ = 2
VMEM_LIMIT_BYTES = 56 * 1024 * 1024

GROUP_A_HEADS = (0, 2, 5, 7)
GROUP_B_HEADS = (1, 3, 4, 6)


def _rms_scale(xf, width):
    return lax.rsqrt(jnp.sum(xf * xf, axis=-1, keepdims=True) * (1.0 / width) + EPS)


def _dot(a, b):
    return jnp.dot(a, b, preferred_element_type=jnp.float32)


def _layer_kernel(tiles_per_seq, n_tiles, sinks_ref, x_ref, *refs):
    step = pl.program_id(0)
    kv_tail, u_tail = refs[-3], refs[-2]

    @pl.when(step == 0)
    def _():
        kv_tail[...] = jnp.zeros_like(kv_tail)
        u_tail[...] = jnp.zeros_like(u_tail)
        _step(True, False, True, sinks_ref, x_ref, *refs)

    @pl.when(jnp.logical_and(step > 0, step < n_tiles))
    def _():
        _step(True, True, (step % tiles_per_seq) == 0, sinks_ref, x_ref, *refs)

    @pl.when(step == n_tiles)
    def _():
        _step(False, True, False, sinks_ref, x_ref, *refs)


def _step(with_mixer, with_mlp, first, sinks_ref, x_ref, g1_ref, win_ref, gq_ref, gk_ref,
          eq_ref, ek_ref, cw_ref, gc_ref, ga_ref, wo_ref, g2_ref, wu_ref, wd_ref,
          o_ref, x1_prev, h_prev, kv_tail, u_tail, ybuf):
    tm = TOKEN_TILE

    if with_mlp:
        o_ref[...] = x1_prev[...]
    n_blk = FF_CHUNK // MXU_COLS
    acts = {}

    def up_piece(c, n):
        w0 = c * FF_CHUNK + n * MXU_COLS
        a = jnp.maximum(_dot(h_prev[...], wu_ref[:, w0:w0 + MXU_COLS]), 0.0)
        acts[c, n] = (a * a).astype(jnp.bfloat16)

    def down_piece(c, n):
        a = jnp.concatenate([acts[c, i] for i in range(n_blk)], axis=1)
        o_ref[:, n * MXU_COLS:(n + 1) * MXU_COLS] += _dot(
            a, wd_ref[c * FF_CHUNK:(c + 1) * FF_CHUNK, n * MXU_COLS:(n + 1) * MXU_COLS])

    pieces = [functools.partial(f, c, n) for c in range(D_FF // FF_CHUNK)
              for f in (up_piece, down_piece) for n in range(n_blk)] if with_mlp else []

    def mlp_pieces(count):
        for _ in range(min(count, len(pieces))):
            pieces.pop(0)()

    if not with_mixer:
        mlp_pieces(len(pieces))
        return
    mlp_pieces(4)

    x = x_ref[...]
    h = (x * _rms_scale(x, D_MODEL) * g1_ref[...]).astype(jnp.bfloat16)
    o0 = ATTN_WIDTH
    o1 = o0 + 2 * KV_WIDTH
    zq = _dot(h, win_ref[:, :o0])
    zkv = _dot(h, win_ref[:, o0:o1])
    k = zkv[:, :KV_WIDTH]
    v = zkv[:, KV_WIDTH:]
    half = ATTN_WIDTH // 2
    q2 = (zq * zq).astype(jnp.bfloat16)
    q_ms = jnp.concatenate([_dot(q2[:, :half], eq_ref[...]), _dot(q2[:, half:], eq_ref[...])], axis=1)
    k_ms = _dot((k * k).astype(jnp.bfloat16), ek_ref[...])
    zc = _dot(h, win_ref[:, o1:])
    qn = zq * lax.rsqrt(q_ms + EPS) * gq_ref[...]
    kn = k * lax.rsqrt(k_ms + EPS) * gk_ref[...]
    col = lax.broadcasted_iota(jnp.int32, (tm, ATTN_WIDTH), 1)
    in_a = ((col % LANES) < HEAD_DIM) == (col < half)
    qa = jnp.where(in_a, qn, 0.0).astype(jnp.bfloat16)
    qb = jnp.where(in_a, 0.0, qn).astype(jnp.bfloat16)
    bf = lambda t: t.astype(jnp.bfloat16)
    kv = jnp.concatenate([bf(kn), bf(pltpu.roll(kn, HEAD_DIM, 1)),
                          bf(v), bf(pltpu.roll(v, HEAD_DIM, 1))], axis=1)
    kv_prev = kv_tail[...]
    kv_tail[...] = kv[tm - WINDOW:]

    u = zc[:, CONV_WIDTH:2 * CONV_WIDTH] * zc[:, 2 * CONV_WIDTH:]
    ext = jnp.concatenate([jnp.where(first, 0.0, u_tail[...]), u], axis=0)
    u1 = pltpu.roll(ext, 1, 0)[SUBLANES:]
    u2 = pltpu.roll(ext, 2, 0)[SUBLANES:]
    u_tail[...] = u[tm - SUBLANES:]
    conv = cw_ref[0:1, :] * u2 + cw_ref[1:2, :] * u1 + cw_ref[2:3, :] * u
    yc = zc[:, :CONV_WIDTH] * conv
    ybuf[:, ATTN_WIDTH:] = (yc * _rms_scale(yc, CONV_WIDTH) * gc_ref[...]).astype(jnp.bfloat16)

    lane = lax.broadcasted_iota(jnp.int32, (WINDOW, LANES), 1)
    lo = lane < HEAD_DIM
    qi = lax.broadcasted_iota(jnp.int32, (WINDOW, 2 * WINDOW), 0)
    kj = lax.broadcasted_iota(jnp.int32, (WINDOW, 2 * WINDOW), 1)
    band = (kj > qi) & (kj <= qi + WINDOW)
    band_first = band & (kj >= jnp.where(first, WINDOW, 0))
    band4 = jnp.concatenate([band] * 4, axis=0)
    band4_first = jnp.concatenate([band_first] * 4, axis=0)

    def sink_rows(heads):
        return jnp.concatenate(
            [jnp.full((WINDOW, LANES), sinks_ref[hd], jnp.float32) for hd in heads], axis=0)

    sink_a = sink_rows(GROUP_A_HEADS)
    sink_b = sink_rows(GROUP_B_HEADS)
    ones = jnp.ones((2 * WINDOW, LANES), jnp.bfloat16)

    def scores(qg, r0, kk, sink, mask):
        qs = jnp.concatenate(
            [qg[r0:r0 + WINDOW, sl * LANES:(sl + 1) * LANES] for sl in range(4)], axis=0)
        s = lax.dot_general(qs, kk, (((1,), (1,)), ((), ())),
                            preferred_element_type=jnp.float32)
        s = jnp.where(mask, s, NEG_INF)
        m = jnp.max(jnp.maximum(s[:, :LANES], s[:, LANES:]), axis=-1, keepdims=True)
        m = jnp.maximum(jnp.broadcast_to(m, sink.shape), sink)
        p = jnp.exp2(s - jnp.concatenate([m, m], axis=1))
        return p.astype(jnp.bfloat16), m

    def weighted(p, m, vv, sink):
        ov = _dot(p, jnp.concatenate([vv, ones], axis=1))
        l = ov[:, LANES:] + jnp.exp2(sink - m)
        return ov[:, :LANES] / l

    def kv_window(j):
        if j == 0:
            return jnp.concatenate([kv_prev, kv[0:WINDOW]], axis=0)
        return kv[(j - 1) * WINDOW:(j + 1) * WINDOW]

    def block_scores(j):
        kvw = kv_window(j)
        mask = band4_first if j == 0 else band4
        return (scores(qa, j * WINDOW, kvw[:, 0:LANES], sink_a, mask),
                scores(qb, j * WINDOW, kvw[:, LANES:2 * LANES], sink_b, mask))

    def block_output(j, pm):
        kvw = kv_window(j)
        (p_a, m_a), (p_b, m_b) = pm
        o_a = weighted(p_a, m_a, kvw[:, 2 * LANES:3 * LANES], sink_a)
        o_b = weighted(p_b, m_b, kvw[:, 3 * LANES:4 * LANES], sink_b)
        rows = lambda o, sl: o[sl * WINDOW:(sl + 1) * WINDOW]
        y = jnp.concatenate([jnp.where(lo, rows(o_a, 0), rows(o_b, 0)),
                             jnp.where(lo, rows(o_a, 1), rows(o_b, 1)),
                             jnp.where(lo, rows(o_b, 2), rows(o_a, 2)),
                             jnp.where(lo, rows(o_b, 3), rows(o_a, 3))], axis=1)
        y = y * _rms_scale(y, ATTN_WIDTH) * ga_ref[...]
        ybuf[j * WINDOW:(j + 1) * WINDOW, 0:ATTN_WIDTH] = y.astype(jnp.bfloat16)

    n_blocks = tm // WINDOW
    mlp_pieces(MLP_PIECES_BEFORE_ATTENTION)
    pending = block_scores(0)
    for j in range(n_blocks):
        upcoming = block_scores(j + 1) if j + 1 < n_blocks else None
        mlp_pieces(MLP_PIECES_PER_BLOCK)
        block_output(j, pending)
        pending = upcoming
    x1 = x + _dot(ybuf[:, ATTN_WIDTH:], wo_ref[ATTN_WIDTH:, :])
    mlp_pieces(MLP_PIECES_BEFORE_OUT_PROJ)
    x1 = x1 + _dot(ybuf[:, :ATTN_WIDTH], wo_ref[:ATTN_WIDTH, :])
    mlp_pieces(len(pieces))

    x1_prev[...] = x1
    h_prev[...] = (x1 * _rms_scale(x1, D_MODEL) * g2_ref[...]).astype(jnp.bfloat16)


def _layer(x2, sinks, g1, w_in, gq, gk, eq, ek, conv_w, gc, ga, w_out, g2, w_up, w_down, seq_len):
    t = x2.shape[0]
    tm = TOKEN_TILE
    n_tiles = t // tm
    const = lambda shape: pl.BlockSpec(
        shape, lambda i, *_: (0,) * len(shape), pipeline_mode=pl.Buffered(1))
    grid_spec = pltpu.PrefetchScalarGridSpec(
        num_scalar_prefetch=1,
        grid=(n_tiles + 1,),
        in_specs=[
            pl.BlockSpec((tm, D_MODEL), lambda i, *_: (jnp.minimum(i, n_tiles - 1), 0)),
            const((1, D_MODEL)), const((D_MODEL, IN_COLS)), const((1, ATTN_WIDTH)),
            const((1, KV_WIDTH)), const((ATTN_WIDTH // 2, ATTN_WIDTH // 2)),
            const((KV_WIDTH, KV_WIDTH)), const((CONV_K, CONV_WIDTH)), const((1, CONV_WIDTH)),
            const((1, ATTN_WIDTH)), const((D_MODEL, D_MODEL)), const((1, D_MODEL)),
            const((D_MODEL, D_FF)), const((D_FF, D_MODEL)),
        ],
        out_specs=pl.BlockSpec((tm, D_MODEL), lambda i, *_: (jnp.maximum(i - 1, 0), 0)),
        scratch_shapes=[
            pltpu.VMEM((tm, D_MODEL), jnp.float32),
            pltpu.VMEM((tm, D_MODEL), jnp.bfloat16),
            pltpu.VMEM((WINDOW, 4 * LANES), jnp.bfloat16),
            pltpu.VMEM((SUBLANES, CONV_WIDTH), jnp.float32),
            pltpu.VMEM((tm, D_MODEL), jnp.bfloat16),
        ],
    )
    return pl.pallas_call(
        functools.partial(_layer_kernel, seq_len // tm, n_tiles),
        out_shape=jax.ShapeDtypeStruct((t, D_MODEL), jnp.float32),
        grid_spec=grid_spec,
        compiler_params=pltpu.CompilerParams(
            dimension_semantics=("arbitrary",), vmem_limit_bytes=VMEM_LIMIT_BYTES),
        name="layer",
    )(sinks, x2, g1, w_in, gq, gk, eq, ek, conv_w, gc, ga, w_out, g2, w_up, w_down)


def _head_mean_matrix(width):
    head = jnp.arange(width) // HEAD_DIM
    return jnp.where(head[:, None] == head[None, :], 1.0 / HEAD_DIM, 0.0).astype(jnp.bfloat16)


def kernel(x, attn_norm_g, w_in, q_norm_g, k_norm_g, sinks, conv_w, attn_out_g, conv_out_g,
           w_out, mlp_norm_g, w_up, w_down):
    b, s, d = x.shape
    depth = w_in.shape[0]
    assert d == D_MODEL and s % TOKEN_TILE == 0
    bf16 = jnp.bfloat16
    x2 = x.reshape(b * s, d)
    eq = _head_mean_matrix(ATTN_WIDTH // 2)
    ek = _head_mean_matrix(KV_WIDTH)
    for l in range(depth):
        gq = jnp.tile(q_norm_g[l], N_HEADS)[None, :] * (HEAD_DIM ** -0.5 * LOG2E)
        gk = jnp.tile(k_norm_g[l], N_KV_HEADS)[None, :]
        x2 = _layer(x2, sinks[l] * LOG2E, attn_norm_g[l][None, :], w_in[l].astype(bf16), gq, gk,
                    eq, ek, conv_w[l], conv_out_g[l][None, :], attn_out_g[l][None, :],
                    w_out[l].astype(bf16), mlp_norm_g[l][None, :], w_up[l].astype(bf16),
                    w_down[l].astype(bf16), s)
    return x2.reshape(b, s, d)
```

```python
import functools
import math

import jax
import jax.numpy as jnp
from jax import lax
from jax.experimental import pallas as pl
from jax.experimental.pallas import tpu as pltpu

D_MODEL = 1024
N_HEADS = 8
HEAD_DIM = 64
N_KV_HEADS = 2
ATTN_WIDTH = N_HEADS * HEAD_DIM
KV_WIDTH = N_KV_HEADS * HEAD_DIM
CONV_WIDTH = D_MODEL - ATTN_WIDTH
CONV_K = 3
WINDOW = 128
IN_COLS = ATTN_WIDTH + 2 * KV_WIDTH + 3 * CONV_WIDTH
D_FF = 4 * D_MODEL
EPS = 1e-6
NEG_INF = -1e30
LOG2E = math.log2(math.e)

LANES = 128
SUBLANES = 8
TOKEN_TILE = 512
FF_CHUNK = 1024
MXU_COLS = 256
MLP_PIECES_BEFORE_ATTENTION = 2
MLP_PIECES_PER_BLOCK = 2
MLP_PIECES_BEFORE_OUT_PROJ = 2
WEIGHT_STAGE_BLOCK = (512, 1024)
VMEM_LIMIT_BYTES = 56 * 1024 * 1024

GROUP_A_HEADS = (0, 2, 5, 7)
GROUP_B_HEADS = (1, 3, 4, 6)


def _rms_scale(xf, width):
    return lax.rsqrt(jnp.sum(xf * xf, axis=-1, keepdims=True) * (1.0 / width) + EPS)


def _dot(a, b):
    return jnp.dot(a, b, preferred_element_type=jnp.float32)


def _load_weights(hbm_refs, vmem_refs, stage, sem):
    rows, cols = stage.shape[1:]
    jobs = [(src, dst, r0, c0, min(cols, src.shape[1] - c0))
            for src, dst in zip(hbm_refs, vmem_refs)
            for r0 in range(0, src.shape[0], rows) for c0 in range(0, src.shape[1], cols)]

    def copy(j):
        src, _, r0, c0, nc = jobs[j]
        return pltpu.make_async_copy(src.at[pl.ds(r0, rows), pl.ds(c0, nc)],
                                     stage.at[j % 2, :, pl.ds(0, nc)], sem.at[j % 2])

    copy(0).start()
    copy(1).start()
    for j, (_, dst, r0, c0, nc) in enumerate(jobs):
        copy(j).wait()
        dst[r0:r0 + rows, c0:c0 + nc] = stage[j % 2, :, 0:nc].astype(jnp.bfloat16)
        if j + 2 < len(jobs):
            copy(j + 2).start()


def _layer_kernel(tiles_per_seq, n_tiles, sinks_ref, x_ref, g1_ref, gq_ref, gk_ref, eq_ref, ek_ref,
                  cw_ref, gc_ref, ga_ref, g2_ref, win_hbm, wo_hbm, wu_hbm, wd_hbm, o_ref,
                  x1_prev, h_prev, kv_tail, u_tail, ybuf, win_ref, wo_ref, wu_ref, wd_ref,
                  stage, sem):
    step = pl.program_id(0)
    refs = (sinks_ref, x_ref, g1_ref, win_ref, gq_ref, gk_ref, eq_ref, ek_ref, cw_ref, gc_ref,
            ga_ref, wo_ref, g2_ref, wu_ref, wd_ref, o_ref, x1_prev, h_prev, kv_tail, u_tail, ybuf)

    @pl.when(step == 0)
    def _():
        kv_tail[...] = jnp.zeros_like(kv_tail)
        u_tail[...] = jnp.zeros_like(u_tail)
        _load_weights((win_hbm, wo_hbm, wu_hbm, wd_hbm), (win_ref, wo_ref, wu_ref, wd_ref),
                      stage, sem)
        _step(True, False, True, *refs)

    @pl.when(jnp.logical_and(step > 0, step < n_tiles))
    def _():
        _step(True, True, (step % tiles_per_seq) == 0, *refs)

    @pl.when(step == n_tiles)
    def _():
        _step(False, True, False, *refs)


def _step(with_mixer, with_mlp, first, sinks_ref, x_ref, g1_ref, win_ref, gq_ref, gk_ref,
          eq_ref, ek_ref, cw_ref, gc_ref, ga_ref, wo_ref, g2_ref, wu_ref, wd_ref,
          o_ref, x1_prev, h_prev, kv_tail, u_tail, ybuf):
    tm = TOKEN_TILE

    if with_mlp:
        o_ref[...] = x1_prev[...]
    n_blk = FF_CHUNK // MXU_COLS
    acts = {}

    def up_piece(c, n):
        w0 = c * FF_CHUNK + n * MXU_COLS
        a = jnp.maximum(_dot(h_prev[...], wu_ref[:, w0:w0 + MXU_COLS]), 0.0)
        acts[c, n] = (a * a).astype(jnp.bfloat16)

    def down_piece(c, n):
        a = jnp.concatenate([acts[c, i] for i in range(n_blk)], axis=1)
        o_ref[:, n * MXU_COLS:(n + 1) * MXU_COLS] += _dot(
            a, wd_ref[c * FF_CHUNK:(c + 1) * FF_CHUNK, n * MXU_COLS:(n + 1) * MXU_COLS])

    pieces = [functools.partial(f, c, n) for c in range(D_FF // FF_CHUNK)
              for f in (up_piece, down_piece) for n in range(n_blk)] if with_mlp else []

    def mlp_pieces(count):
        for _ in range(min(count, len(pieces))):
            pieces.pop(0)()

    if not with_mixer:
        mlp_pieces(len(pieces))
        return
    mlp_pieces(4)

    x = x_ref[...]
    h = (x * _rms_scale(x, D_MODEL) * g1_ref[...]).astype(jnp.bfloat16)
    o0 = ATTN_WIDTH
    o1 = o0 + 2 * KV_WIDTH
    zq = _dot(h, win_ref[:, :o0])
    zkv = _dot(h, win_ref[:, o0:o1])
    k = zkv[:, :KV_WIDTH]
    v = zkv[:, KV_WIDTH:]
    half = ATTN_WIDTH // 2
    q2 = (zq * zq).astype(jnp.bfloat16)
    q_ms = jnp.concatenate([_dot(q2[:, :half], eq_ref[...]), _dot(q2[:, half:], eq_ref[...])], axis=1)
    k_ms = _dot((k * k).astype(jnp.bfloat16), ek_ref[...])
    zc = _dot(h, win_ref[:, o1:])
    qn = zq * lax.rsqrt(q_ms + EPS) * gq_ref[...]
    kn = k * lax.rsqrt(k_ms + EPS) * gk_ref[...]
    col = lax.broadcasted_iota(jnp.int32, (tm, ATTN_WIDTH), 1)
    in_a = ((col % LANES) < HEAD_DIM) == (col < half)
    qa = jnp.where(in_a, qn, 0.0).astype(jnp.bfloat16)
    qb = jnp.where(in_a, 0.0, qn).astype(jnp.bfloat16)
    bf = lambda t: t.astype(jnp.bfloat16)
    kv = jnp.concatenate([bf(kn), bf(pltpu.roll(kn, HEAD_DIM, 1)),
                          bf(v), bf(pltpu.roll(v, HEAD_DIM, 1))], axis=1)
    kv_prev = kv_tail[...]
    kv_tail[...] = kv[tm - WINDOW:]

    u = zc[:, CONV_WIDTH:2 * CONV_WIDTH] * zc[:, 2 * CONV_WIDTH:]
    ext = jnp.concatenate([jnp.where(first, 0.0, u_tail[...]), u], axis=0)
    u1 = pltpu.roll(ext, 1, 0)[SUBLANES:]
    u2 = pltpu.roll(ext, 2, 0)[SUBLANES:]
    u_tail[...] = u[tm - SUBLANES:]
    conv = cw_ref[0:1, :] * u2 + cw_ref[1:2, :] * u1 + cw_ref[2:3, :] * u
    yc = zc[:, :CONV_WIDTH] * conv
    ybuf[:, ATTN_WIDTH:] = (yc * _rms_scale(yc, CONV_WIDTH) * gc_ref[...]).astype(jnp.bfloat16)

    lane = lax.broadcasted_iota(jnp.int32, (WINDOW, LANES), 1)
    lo = lane < HEAD_DIM
    qi = lax.broadcasted_iota(jnp.int32, (WINDOW, 2 * WINDOW), 0)
    kj = lax.broadcasted_iota(jnp.int32, (WINDOW, 2 * WINDOW), 1)
    band = (kj > qi) & (kj <= qi + WINDOW)
    band_first = band & (kj >= jnp.where(first, WINDOW, 0))
    band4 = jnp.concatenate([band] * 4, axis=0)
    band4_first = jnp.concatenate([band_first] * 4, axis=0)

    def sink_rows(heads):
        return jnp.concatenate(
            [jnp.full((WINDOW, LANES), sinks_ref[hd], jnp.float32) for hd in heads], axis=0)

    sink_a = sink_rows(GROUP_A_HEADS)
    sink_b = sink_rows(GROUP_B_HEADS)
    ones = jnp.ones((2 * WINDOW, LANES), jnp.bfloat16)

    def scores(qg, r0, kk, sink, mask):
        qs = jnp.concatenate(
            [qg[r0:r0 + WINDOW, sl * LANES:(sl + 1) * LANES] for sl in range(4)], axis=0)
        s = lax.dot_general(qs, kk, (((1,), (1,)), ((), ())),
                            preferred_element_type=jnp.float32)
        s = jnp.where(mask, s, NEG_INF)
        m = jnp.max(jnp.maximum(s[:, :LANES], s[:, LANES:]), axis=-1, keepdims=True)
        m = jnp.maximum(jnp.broadcast_to(m, sink.shape), sink)
        p = jnp.exp2(s - jnp.concatenate([m, m], axis=1))
        return p.astype(jnp.bfloat16), m

    def weighted(p, m, vv, sink):
        ov = _dot(p, jnp.concatenate([vv, ones], axis=1))
        l = ov[:, LANES:] + jnp.exp2(sink - m)
        return ov[:, :LANES] / l

    def kv_window(j):
        if j == 0:
            return jnp.concatenate([kv_prev, kv[0:WINDOW]], axis=0)
        return kv[(j - 1) * WINDOW:(j + 1) * WINDOW]

    def block_scores(j):
        kvw = kv_window(j)
        mask = band4_first if j == 0 else band4
        return (scores(qa, j * WINDOW, kvw[:, 0:LANES], sink_a, mask),
                scores(qb, j * WINDOW, kvw[:, LANES:2 * LANES], sink_b, mask))

    def block_output(j, pm):
        kvw = kv_window(j)
        (p_a, m_a), (p_b, m_b) = pm
        o_a = weighted(p_a, m_a, kvw[:, 2 * LANES:3 * LANES], sink_a)
        o_b = weighted(p_b, m_b, kvw[:, 3 * LANES:4 * LANES], sink_b)
        rows = lambda o, sl: o[sl * WINDOW:(sl + 1) * WINDOW]
        y = jnp.concatenate([jnp.where(lo, rows(o_a, 0), rows(o_b, 0)),
                             jnp.where(lo, rows(o_a, 1), rows(o_b, 1)),
                             jnp.where(lo, rows(o_b, 2), rows(o_a, 2)),
                             jnp.where(lo, rows(o_b, 3), rows(o_a, 3))], axis=1)
        y = y * _rms_scale(y, ATTN_WIDTH) * ga_ref[...]
        ybuf[j * WINDOW:(j + 1) * WINDOW, 0:ATTN_WIDTH] = y.astype(jnp.bfloat16)

    n_blocks = tm // WINDOW
    mlp_pieces(MLP_PIECES_BEFORE_ATTENTION)
    pending = block_scores(0)
    for j in range(n_blocks):
        upcoming = block_scores(j + 1) if j + 1 < n_blocks else None
        mlp_pieces(MLP_PIECES_PER_BLOCK)
        block_output(j, pending)
        pending = upcoming
    x1 = x + _dot(ybuf[:, ATTN_WIDTH:], wo_ref[ATTN_WIDTH:, :])
    mlp_pieces(MLP_PIECES_BEFORE_OUT_PROJ)
    x1 = x1 + _dot(ybuf[:, :ATTN_WIDTH], wo_ref[:ATTN_WIDTH, :])
    mlp_pieces(len(pieces))

    x1_prev[...] = x1
    h_prev[...] = (x1 * _rms_scale(x1, D_MODEL) * g2_ref[...]).astype(jnp.bfloat16)


def _layer(x2, sinks, g1, w_in, gq, gk, eq, ek, conv_w, gc, ga, w_out, g2, w_up, w_down, seq_len):
    t = x2.shape[0]
    tm = TOKEN_TILE
    n_tiles = t // tm
    const = lambda shape: pl.BlockSpec(
        shape, lambda i, *_: (0,) * len(shape), pipeline_mode=pl.Buffered(1))
    grid_spec = pltpu.PrefetchScalarGridSpec(
        num_scalar_prefetch=1,
        grid=(n_tiles + 1,),
        in_specs=[
            pl.BlockSpec((tm, D_MODEL), lambda i, *_: (jnp.minimum(i, n_tiles - 1), 0)),
            const((1, D_MODEL)), const((1, ATTN_WIDTH)), const((1, KV_WIDTH)),
            const((ATTN_WIDTH // 2, ATTN_WIDTH // 2)), const((KV_WIDTH, KV_WIDTH)),
            const((CONV_K, CONV_WIDTH)), const((1, CONV_WIDTH)), const((1, ATTN_WIDTH)),
            const((1, D_MODEL)),
        ] + [pl.BlockSpec(memory_space=pl.ANY)] * 4,
        out_specs=pl.BlockSpec((tm, D_MODEL), lambda i, *_: (jnp.maximum(i - 1, 0), 0)),
        scratch_shapes=[
            pltpu.VMEM((tm, D_MODEL), jnp.float32),
            pltpu.VMEM((tm, D_MODEL), jnp.bfloat16),
            pltpu.VMEM((WINDOW, 4 * LANES), jnp.bfloat16),
            pltpu.VMEM((SUBLANES, CONV_WIDTH), jnp.float32),
            pltpu.VMEM((tm, D_MODEL), jnp.bfloat16),
            pltpu.VMEM((D_MODEL, IN_COLS), jnp.bfloat16),
            pltpu.VMEM((D_MODEL, D_MODEL), jnp.bfloat16),
            pltpu.VMEM((D_MODEL, D_FF), jnp.bfloat16),
            pltpu.VMEM((D_FF, D_MODEL), jnp.bfloat16),
            pltpu.VMEM((2,) + WEIGHT_STAGE_BLOCK, jnp.float32),
            pltpu.SemaphoreType.DMA((2,)),
        ],
    )
    return pl.pallas_call(
        functools.partial(_layer_kernel, seq_len // tm, n_tiles),
        out_shape=jax.ShapeDtypeStruct((t, D_MODEL), jnp.float32),
        grid_spec=grid_spec,
        compiler_params=pltpu.CompilerParams(
            dimension_semantics=("arbitrary",), vmem_limit_bytes=VMEM_LIMIT_BYTES),
        name="layer",
    )(sinks, x2, g1, gq, gk, eq, ek, conv_w, gc, ga, g2, w_in, w_out, w_up, w_down)


def _head_mean_matrix(width):
    head = jnp.arange(width) // HEAD_DIM
    return jnp.where(head[:, None] == head[None, :], 1.0 / HEAD_DIM, 0.0).astype(jnp.bfloat16)


def kernel(x, attn_norm_g, w_in, q_norm_g, k_norm_g, sinks, conv_w, attn_out_g, conv_out_g,
           w_out, mlp_norm_g, w_up, w_down):
    b, s, d = x.shape
    depth = w_in.shape[0]
    assert d == D_MODEL and s % TOKEN_TILE == 0
    x2 = x.reshape(b * s, d)
    eq = _head_mean_matrix(ATTN_WIDTH // 2)
    ek = _head_mean_matrix(KV_WIDTH)
    for l in range(depth):
        gq = jnp.tile(q_norm_g[l], N_HEADS)[None, :] * (HEAD_DIM ** -0.5 * LOG2E)
        gk = jnp.tile(k_norm_g[l], N_KV_HEADS)[None, :]
        x2 = _layer(x2, sinks[l] * LOG2E, attn_norm_g[l][None, :], w_in[l], gq, gk,
                    eq, ek, conv_w[l], conv_out_g[l][None, :], attn_out_g[l][None, :],
                    w_out[l], mlp_norm_g[l][None, :], w_up[l], w_down[l], s)
    return x2.reshape(b, s, d)
```

```python
import functools
import math

import jax
import jax.numpy as jnp
from jax import lax
from jax.experimental import pallas as pl
from jax.experimental.pallas import tpu as pltpu

D_MODEL = 1024
N_HEADS = 8
HEAD_DIM = 64
N_KV_HEADS = 2
ATTN_WIDTH = N_HEADS * HEAD_DIM
KV_WIDTH = N_KV_HEADS * HEAD_DIM
CONV_WIDTH = D_MODEL - ATTN_WIDTH
CONV_K = 3
WINDOW = 128
IN_COLS = ATTN_WIDTH + 2 * KV_WIDTH + 3 * CONV_WIDTH
D_FF = 4 * D_MODEL
EPS = 1e-6
NEG_INF = -1e30
LOG2E = math.log2(math.e)

LANES = 128
SUBLANES = 8
TOKEN_TILE = 512
FF_CHUNK = 1024
MXU_COLS = 256
MLP_PIECES_BEFORE_ATTENTION = 2
MLP_PIECES_PER_BLOCK = 2
MLP_PIECES_BEFORE_OUT_PROJ = 2
WEIGHT_STAGE_BLOCK = (512, 1024)
VMEM_LIMIT_BYTES = 56 * 1024 * 1024

GROUP_A_HEADS = (0, 2, 5, 7)
GROUP_B_HEADS = (1, 3, 4, 6)


def _rms_scale(xf, width):
    return lax.rsqrt(jnp.sum(xf * xf, axis=-1, keepdims=True) * (1.0 / width) + EPS)


def _dot(a, b):
    return jnp.dot(a, b, preferred_element_type=jnp.float32)


def _weight_jobs(hbm_refs, vmem_refs, stage, sem):
    rows, cols = stage.shape[1:]
    jobs = [(src, dst, r0, c0, min(cols, src.shape[1] - c0))
            for src, dst in zip(hbm_refs, vmem_refs)
            for r0 in range(0, src.shape[0], rows) for c0 in range(0, src.shape[1], cols)]

    def copy(j):
        src, _, r0, c0, nc = jobs[j]
        return pltpu.make_async_copy(src.at[pl.ds(r0, rows), pl.ds(c0, nc)],
                                     stage.at[j % 2, :, pl.ds(0, nc)], sem.at[j % 2])

    def run(j):
        _, dst, r0, c0, nc = jobs[j]
        copy(j).wait()
        dst[r0:r0 + rows, c0:c0 + nc] = stage[j % 2, :, 0:nc].astype(jnp.bfloat16)
        if j + 2 < len(jobs):
            copy(j + 2).start()

    copy(0).start()
    copy(1).start()
    return [functools.partial(run, j) for j in range(len(jobs))]


def _head_mean_matrix(width):
    row = lax.broadcasted_iota(jnp.int32, (width, width), 0) // HEAD_DIM
    col = lax.broadcasted_iota(jnp.int32, (width, width), 1) // HEAD_DIM
    return jnp.where(row == col, 1.0 / HEAD_DIM, 0.0).astype(jnp.bfloat16)


def _layer_kernel(tiles_per_seq, n_tiles, sinks_ref, x_ref, g1_ref, gq_ref, gk_ref, cw_ref, gc_ref,
                  ga_ref, g2_ref, win_hbm, wo_hbm, wu_hbm, wd_hbm, o_ref,
                  x1_prev, h_prev, kv_tail, u_tail, ybuf, eq_ref, ek_ref, win_ref, wo_ref, wu_ref,
                  wd_ref, stage, sem):
    step = pl.program_id(0)
    refs = (sinks_ref, x_ref, g1_ref, win_ref, gq_ref, gk_ref, eq_ref, ek_ref, cw_ref, gc_ref,
            ga_ref, wo_ref, g2_ref, wu_ref, wd_ref, o_ref, x1_prev, h_prev, kv_tail, u_tail, ybuf)

    @pl.when(step == 0)
    def _():
        kv_tail[...] = jnp.zeros_like(kv_tail)
        u_tail[...] = jnp.zeros_like(u_tail)
        eq_ref[...] = _head_mean_matrix(eq_ref.shape[0])
        ek_ref[...] = _head_mean_matrix(ek_ref.shape[0])
        for job in _weight_jobs((win_hbm, wo_hbm), (win_ref, wo_ref), stage, sem):
            job()
        _step(True, _weight_jobs((wu_hbm, wd_hbm), (wu_ref, wd_ref), stage, sem), True, *refs)

    @pl.when(jnp.logical_and(step > 0, step < n_tiles))
    def _():
        _step(True, None, (step % tiles_per_seq) == 0, *refs)

    @pl.when(step == n_tiles)
    def _():
        _step(False, None, False, *refs)


def _step(with_mixer, fillers, first, sinks_ref, x_ref, g1_ref, win_ref, gq_ref, gk_ref,
          eq_ref, ek_ref, cw_ref, gc_ref, ga_ref, wo_ref, g2_ref, wu_ref, wd_ref,
          o_ref, x1_prev, h_prev, kv_tail, u_tail, ybuf):
    tm = TOKEN_TILE
    with_mlp = fillers is None

    if with_mlp:
        o_ref[...] = x1_prev[...]
    n_up = FF_CHUNK // MXU_COLS
    n_down = D_MODEL // MXU_COLS
    acts = {}

    def up_piece(c, n):
        w0 = c * FF_CHUNK + n * MXU_COLS
        a = jnp.maximum(_dot(h_prev[...], wu_ref[:, w0:w0 + MXU_COLS]), 0.0)
        acts[c, n] = (a * a).astype(jnp.bfloat16)

    def down_piece(c, n):
        a = jnp.concatenate([acts[c, i] for i in range(n_up)], axis=1)
        o_ref[:, n * MXU_COLS:(n + 1) * MXU_COLS] += _dot(
            a, wd_ref[c * FF_CHUNK:(c + 1) * FF_CHUNK, n * MXU_COLS:(n + 1) * MXU_COLS])

    pieces = list(fillers) if fillers is not None else [
        functools.partial(f, c, n) for c in range(D_FF // FF_CHUNK)
        for f, count in ((up_piece, n_up), (down_piece, n_down)) for n in range(count)]

    def mlp_pieces(count):
        for _ in range(min(count, len(pieces))):
            pieces.pop(0)()

    if not with_mixer:
        mlp_pieces(len(pieces))
        return
    mlp_pieces(4)

    x = x_ref[...]
    h = (x * _rms_scale(x, D_MODEL) * g1_ref[...]).astype(jnp.bfloat16)
    o0 = ATTN_WIDTH
    o1 = o0 + 2 * KV_WIDTH
    zq = _dot(h, win_ref[:, :o0])
    zkv = _dot(h, win_ref[:, o0:o1])
    k = zkv[:, :KV_WIDTH]
    v = zkv[:, KV_WIDTH:]
    half = ATTN_WIDTH // 2
    q2 = (zq * zq).astype(jnp.bfloat16)
    q_ms = jnp.concatenate([_dot(q2[:, :half], eq_ref[...]), _dot(q2[:, half:], eq_ref[...])], axis=1)
    k_ms = _dot((k * k).astype(jnp.bfloat16), ek_ref[...])
    zc = _dot(h, win_ref[:, o1:])
    gq = jnp.tile(gq_ref[...], (1, N_HEADS)) * (HEAD_DIM ** -0.5 * LOG2E)
    gk = jnp.tile(gk_ref[...], (1, N_KV_HEADS))
    qn = zq * lax.rsqrt(q_ms + EPS) * gq
    kn = k * lax.rsqrt(k_ms + EPS) * gk
    col = lax.broadcasted_iota(jnp.int32, (tm, ATTN_WIDTH), 1)
    in_a = ((col % LANES) < HEAD_DIM) == (col < half)
    qa = jnp.where(in_a, qn, 0.0).astype(jnp.bfloat16)
    qb = jnp.where(in_a, 0.0, qn).astype(jnp.bfloat16)
    bf = lambda t: t.astype(jnp.bfloat16)
    kv = jnp.concatenate([bf(kn), bf(pltpu.roll(kn, HEAD_DIM, 1)),
                          bf(v), bf(pltpu.roll(v, HEAD_DIM, 1))], axis=1)
    kv_prev = kv_tail[...]
    kv_tail[...] = kv[tm - WINDOW:]

    u = zc[:, CONV_WIDTH:2 * CONV_WIDTH] * zc[:, 2 * CONV_WIDTH:]
    ext = jnp.concatenate([jnp.where(first, 0.0, u_tail[...]), u], axis=0)
    u1 = pltpu.roll(ext, 1, 0)[SUBLANES:]
    u2 = pltpu.roll(ext, 2, 0)[SUBLANES:]
    u_tail[...] = u[tm - SUBLANES:]
    conv = cw_ref[0:1, :] * u2 + cw_ref[1:2, :] * u1 + cw_ref[2:3, :] * u
    yc = zc[:, :CONV_WIDTH] * conv
    ybuf[:, ATTN_WIDTH:] = (yc * _rms_scale(yc, CONV_WIDTH) * gc_ref[...]).astype(jnp.bfloat16)

    lane = lax.broadcasted_iota(jnp.int32, (WINDOW, LANES), 1)
    lo = lane < HEAD_DIM
    qi = lax.broadcasted_iota(jnp.int32, (WINDOW, 2 * WINDOW), 0)
    kj = lax.broadcasted_iota(jnp.int32, (WINDOW, 2 * WINDOW), 1)
    band = (kj > qi) & (kj <= qi + WINDOW)
    band_first = band & (kj >= jnp.where(first, WINDOW, 0))
    band4 = jnp.concatenate([band] * 4, axis=0)
    band4_first = jnp.concatenate([band_first] * 4, axis=0)

    def sink_rows(heads):
        return jnp.concatenate(
            [jnp.full((WINDOW, LANES), sinks_ref[hd] * LOG2E, jnp.float32) for hd in heads], axis=0)

    sink_a = sink_rows(GROUP_A_HEADS)
    sink_b = sink_rows(GROUP_B_HEADS)
    ones = jnp.ones((2 * WINDOW, LANES), jnp.bfloat16)

    def scores(qg, r0, kk, sink, mask):
        qs = jnp.concatenate(
            [qg[r0:r0 + WINDOW, sl * LANES:(sl + 1) * LANES] for sl in range(4)], axis=0)
        s = lax.dot_general(qs, kk, (((1,), (1,)), ((), ())),
                            preferred_element_type=jnp.float32)
        s = jnp.where(mask, s, NEG_INF)
        m = jnp.max(jnp.maximum(s[:, :LANES], s[:, LANES:]), axis=-1, keepdims=True)
        m = jnp.maximum(jnp.broadcast_to(m, sink.shape), sink)
        p = jnp.exp2(s - jnp.concatenate([m, m], axis=1))
        return p.astype(jnp.bfloat16), m

    def weighted(p, m, vv, sink):
        ov = _dot(p, jnp.concatenate([vv, ones], axis=1))
        l = ov[:, LANES:] + jnp.exp2(sink - m)
        return ov[:, :LANES] / l

    def kv_window(j):
        if j == 0:
            return jnp.concatenate([kv_prev, kv[0:WINDOW]], axis=0)
        return kv[(j - 1) * WINDOW:(j + 1) * WINDOW]

    def block_scores(j):
        kvw = kv_window(j)
        mask = band4_first if j == 0 else band4
        return (scores(qa, j * WINDOW, kvw[:, 0:LANES], sink_a, mask),
                scores(qb, j * WINDOW, kvw[:, LANES:2 * LANES], sink_b, mask))

    def block_output(j, pm):
        kvw = kv_window(j)
        (p_a, m_a), (p_b, m_b) = pm
        o_a = weighted(p_a, m_a, kvw[:, 2 * LANES:3 * LANES], sink_a)
        o_b = weighted(p_b, m_b, kvw[:, 3 * LANES:4 * LANES], sink_b)
        rows = lambda o, sl: o[sl * WINDOW:(sl + 1) * WINDOW]
        y = jnp.concatenate([jnp.where(lo, rows(o_a, 0), rows(o_b, 0)),
                             jnp.where(lo, rows(o_a, 1), rows(o_b, 1)),
                             jnp.where(lo, rows(o_b, 2), rows(o_a, 2)),
                             jnp.where(lo, rows(o_b, 3), rows(o_a, 3))], axis=1)
        y = y * _rms_scale(y, ATTN_WIDTH) * ga_ref[...]
        ybuf[j * WINDOW:(j + 1) * WINDOW, 0:ATTN_WIDTH] = y.astype(jnp.bfloat16)

    n_blocks = tm // WINDOW
    mlp_pieces(MLP_PIECES_BEFORE_ATTENTION)
    pending = block_scores(0)
    for j in range(n_blocks):
        upcoming = block_scores(j + 1) if j + 1 < n_blocks else None
        mlp_pieces(MLP_PIECES_PER_BLOCK)
        block_output(j, pending)
        pending = upcoming
    x1 = x + _dot(ybuf[:, ATTN_WIDTH:], wo_ref[ATTN_WIDTH:, :])
    mlp_pieces(MLP_PIECES_BEFORE_OUT_PROJ)
    x1 = x1 + _dot(ybuf[:, :ATTN_WIDTH], wo_ref[:ATTN_WIDTH, :])
    mlp_pieces(len(pieces))

    x1_prev[...] = x1
    h_prev[...] = (x1 * _rms_scale(x1, D_MODEL) * g2_ref[...]).astype(jnp.bfloat16)


def _layer(x2, sinks, g1, w_in, gq, gk, conv_w, gc, ga, w_out, g2, w_up, w_down, seq_len):
    t = x2.shape[0]
    tm = TOKEN_TILE
    n_tiles = t // tm
    const = lambda shape: pl.BlockSpec(
        shape, lambda i, *_: (0,) * len(shape), pipeline_mode=pl.Buffered(1))
    grid_spec = pltpu.PrefetchScalarGridSpec(
        num_scalar_prefetch=1,
        grid=(n_tiles + 1,),
        in_specs=[
            pl.BlockSpec((tm, D_MODEL), lambda i, *_: (jnp.minimum(i, n_tiles - 1), 0)),
            const((1, D_MODEL)), const((1, HEAD_DIM)), const((1, HEAD_DIM)),
            const((CONV_K, CONV_WIDTH)), const((1, CONV_WIDTH)), const((1, ATTN_WIDTH)),
            const((1, D_MODEL)),
        ] + [pl.BlockSpec(memory_space=pl.ANY)] * 4,
        out_specs=pl.BlockSpec((tm, D_MODEL), lambda i, *_: (jnp.maximum(i - 1, 0), 0)),
        scratch_shapes=[
            pltpu.VMEM((tm, D_MODEL), jnp.float32),
            pltpu.VMEM((tm, D_MODEL), jnp.bfloat16),
            pltpu.VMEM((WINDOW, 4 * LANES), jnp.bfloat16),
            pltpu.VMEM((SUBLANES, CONV_WIDTH), jnp.float32),
            pltpu.VMEM((tm, D_MODEL), jnp.bfloat16),
            pltpu.VMEM((ATTN_WIDTH // 2, ATTN_WIDTH // 2), jnp.bfloat16),
            pltpu.VMEM((KV_WIDTH, KV_WIDTH), jnp.bfloat16),
            pltpu.VMEM((D_MODEL, IN_COLS), jnp.bfloat16),
            pltpu.VMEM((D_MODEL, D_MODEL), jnp.bfloat16),
            pltpu.VMEM((D_MODEL, D_FF), jnp.bfloat16),
            pltpu.VMEM((D_FF, D_MODEL), jnp.bfloat16),
            pltpu.VMEM((2,) + WEIGHT_STAGE_BLOCK, jnp.float32),
            pltpu.SemaphoreType.DMA((2,)),
        ],
    )
    return pl.pallas_call(
        functools.partial(_layer_kernel, seq_len // tm, n_tiles),
        out_shape=jax.ShapeDtypeStruct((t, D_MODEL), jnp.float32),
        grid_spec=grid_spec,
        compiler_params=pltpu.CompilerParams(
            dimension_semantics=("arbitrary",), vmem_limit_bytes=VMEM_LIMIT_BYTES),
        name="layer",
    )(sinks, x2, g1, gq, gk, conv_w, gc, ga, g2, w_in, w_out, w_up, w_down)


def kernel(x, attn_norm_g, w_in, q_norm_g, k_norm_g, sinks, conv_w, attn_out_g, conv_out_g,
           w_out, mlp_norm_g, w_up, w_down):
    b, s, d = x.shape
    depth = w_in.shape[0]
    assert d == D_MODEL and s % TOKEN_TILE == 0
    x2 = x.reshape(b * s, d)
    for l in range(depth):
        row = lambda p: p[l:l + 1]
        x2 = _layer(x2, sinks[l], row(attn_norm_g), w_in[l], row(q_norm_g), row(k_norm_g),
                    conv_w[l], row(conv_out_g), row(attn_out_g), w_out[l], row(mlp_norm_g),
                    w_up[l], w_down[l], s)
    return x2.reshape(b, s, d)
```

```python
import functools
import math

import jax
import jax.numpy as jnp
from jax import lax
from jax.experimental import pallas as pl
from jax.experimental.pallas import tpu as pltpu

D_MODEL = 1024
N_HEADS = 8
HEAD_DIM = 64
N_KV_HEADS = 2
ATTN_WIDTH = N_HEADS * HEAD_DIM
KV_WIDTH = N_KV_HEADS * HEAD_DIM
CONV_WIDTH = D_MODEL - ATTN_WIDTH
CONV_K = 3
WINDOW = 128
IN_COLS = ATTN_WIDTH + 2 * KV_WIDTH + 3 * CONV_WIDTH
D_FF = 4 * D_MODEL
EPS = 1e-6
NEG_INF = -1e30
LOG2E = math.log2(math.e)

LANES = 128
SUBLANES = 8
TOKEN_TILE = 512
FF_CHUNK = 1024
MXU_COLS = 256
MLP_PIECES_BEFORE_ATTENTION = 2
MLP_PIECES_PER_BLOCK = 2
MLP_PIECES_BEFORE_OUT_PROJ = 2
WEIGHT_STAGE_BLOCK = (512, 1024)
VMEM_LIMIT_BYTES = 56 * 1024 * 1024

GROUP_A_HEADS = (0, 2, 5, 7)
GROUP_B_HEADS = (1, 3, 4, 6)


def _rms_scale(xf, width):
    return lax.rsqrt(jnp.sum(xf * xf, axis=-1, keepdims=True) * (1.0 / width) + EPS)


def _dot(a, b):
    return jnp.dot(a, b, preferred_element_type=jnp.float32)


def _head_mean_square(z):
    lo = lax.broadcasted_iota(jnp.int32, (z.shape[0], LANES), 1) < HEAD_DIM
    out = []
    for s in range(z.shape[1] // LANES):
        sq = z[:, s * LANES:(s + 1) * LANES]
        sq = sq * sq
        lo_sum = jnp.sum(jnp.where(lo, sq, 0.0), axis=-1, keepdims=True)
        hi_sum = jnp.sum(jnp.where(lo, 0.0, sq), axis=-1, keepdims=True)
        out.append(jnp.where(lo, lo_sum, hi_sum) * (1.0 / HEAD_DIM))
    return jnp.concatenate(out, axis=1) if len(out) > 1 else out[0]


def _weight_jobs(hbm_refs, vmem_refs, stage, sem):
    rows, cols = stage.shape[1:]
    jobs = [(src, dst, r0, c0, min(cols, src.shape[1] - c0))
            for src, dst in zip(hbm_refs, vmem_refs)
            for r0 in range(0, src.shape[0], rows) for c0 in range(0, src.shape[1], cols)]

    def copy(j):
        src, _, r0, c0, nc = jobs[j]
        return pltpu.make_async_copy(src.at[pl.ds(r0, rows), pl.ds(c0, nc)],
                                     stage.at[j % 2, :, pl.ds(0, nc)], sem.at[j % 2])

    def run(j):
        _, dst, r0, c0, nc = jobs[j]
        copy(j).wait()
        dst[r0:r0 + rows, c0:c0 + nc] = stage[j % 2, :, 0:nc].astype(jnp.bfloat16)
        if j + 2 < len(jobs):
            copy(j + 2).start()

    copy(0).start()
    copy(1).start()
    return [functools.partial(run, j) for j in range(len(jobs))]


def _layer_kernel(tiles_per_seq, n_tiles, sinks_ref, x_ref, g1_ref, gq_ref, gk_ref, cw_ref, gc_ref,
                  ga_ref, g2_ref, win_hbm, wo_hbm, wu_hbm, wd_hbm, o_ref,
                  x1_prev, h_prev, kv_tail, u_tail, ybuf, win_ref, wo_ref, wu_ref, wd_ref,
                  stage, sem):
    step = pl.program_id(0)
    refs = (sinks_ref, x_ref, g1_ref, win_ref, gq_ref, gk_ref, cw_ref, gc_ref,
            ga_ref, wo_ref, g2_ref, wu_ref, wd_ref, o_ref, x1_prev, h_prev, kv_tail, u_tail, ybuf)

    @pl.when(step == 0)
    def _():
        kv_tail[...] = jnp.zeros_like(kv_tail)
        u_tail[...] = jnp.zeros_like(u_tail)
        for job in _weight_jobs((win_hbm, wo_hbm), (win_ref, wo_ref), stage, sem):
            job()
        _step(True, _weight_jobs((wu_hbm, wd_hbm), (wu_ref, wd_ref), stage, sem), True, *refs)

    @pl.when(jnp.logical_and(step > 0, step < n_tiles))
    def _():
        _step(True, None, (step % tiles_per_seq) == 0, *refs)

    @pl.when(step == n_tiles)
    def _():
        _step(False, None, False, *refs)


def _step(with_mixer, fillers, first, sinks_ref, x_ref, g1_ref, win_ref, gq_ref, gk_ref,
          cw_ref, gc_ref, ga_ref, wo_ref, g2_ref, wu_ref, wd_ref,
          o_ref, x1_prev, h_prev, kv_tail, u_tail, ybuf):
    tm = TOKEN_TILE
    with_mlp = fillers is None

    if with_mlp:
        o_ref[...] = x1_prev[...]
    n_up = FF_CHUNK // MXU_COLS
    n_down = D_MODEL // MXU_COLS
    acts = {}

    def up_piece(c, n):
        w0 = c * FF_CHUNK + n * MXU_COLS
        a = jnp.maximum(_dot(h_prev[...], wu_ref[:, w0:w0 + MXU_COLS]), 0.0)
        acts[c, n] = (a * a).astype(jnp.bfloat16)

    def down_piece(c, n):
        a = jnp.concatenate([acts[c, i] for i in range(n_up)], axis=1)
        o_ref[:, n * MXU_COLS:(n + 1) * MXU_COLS] += _dot(
            a, wd_ref[c * FF_CHUNK:(c + 1) * FF_CHUNK, n * MXU_COLS:(n + 1) * MXU_COLS])

    pieces = list(fillers) if fillers is not None else [
        functools.partial(f, c, n) for c in range(D_FF // FF_CHUNK)
        for f, count in ((up_piece, n_up), (down_piece, n_down)) for n in range(count)]

    def mlp_pieces(count):
        for _ in range(min(count, len(pieces))):
            pieces.pop(0)()

    if not with_mixer:
        mlp_pieces(len(pieces))
        return
    mlp_pieces(4)

    x = x_ref[...]
    h = (x * _rms_scale(x, D_MODEL) * g1_ref[...]).astype(jnp.bfloat16)
    o0 = ATTN_WIDTH
    o1 = o0 + 2 * KV_WIDTH
    zq = _dot(h, win_ref[:, :o0])
    zkv = _dot(h, win_ref[:, o0:o1])
    k = zkv[:, :KV_WIDTH]
    v = zkv[:, KV_WIDTH:]
    half = ATTN_WIDTH // 2
    q_ms = _head_mean_square(zq)
    k_ms = _head_mean_square(k)
    zc = _dot(h, win_ref[:, o1:])
    gq = jnp.tile(gq_ref[...], (1, N_HEADS)) * (HEAD_DIM ** -0.5 * LOG2E)
    gk = jnp.tile(gk_ref[...], (1, N_KV_HEADS))
    qn = zq * lax.rsqrt(q_ms + EPS) * gq
    kn = k * lax.rsqrt(k_ms + EPS) * gk
    col = lax.broadcasted_iota(jnp.int32, (tm, ATTN_WIDTH), 1)
    in_a = ((col % LANES) < HEAD_DIM) == (col < half)
    qa = jnp.where(in_a, qn, 0.0).astype(jnp.bfloat16)
    qb = jnp.where(in_a, 0.0, qn).astype(jnp.bfloat16)
    bf = lambda t: t.astype(jnp.bfloat16)
    kv = jnp.concatenate([bf(kn), bf(pltpu.roll(kn, HEAD_DIM, 1)),
                          bf(v), bf(pltpu.roll(v, HEAD_DIM, 1))], axis=1)
    kv_prev = kv_tail[...]
    kv_tail[...] = kv[tm - WINDOW:]

    u = zc[:, CONV_WIDTH:2 * CONV_WIDTH] * zc[:, 2 * CONV_WIDTH:]
    ext = jnp.concatenate([jnp.where(first, 0.0, u_tail[...]), u], axis=0)
    u1 = pltpu.roll(ext, 1, 0)[SUBLANES:]
    u2 = pltpu.roll(ext, 2, 0)[SUBLANES:]
    u_tail[...] = u[tm - SUBLANES:]
    conv = cw_ref[0:1, :] * u2 + cw_ref[1:2, :] * u1 + cw_ref[2:3, :] * u
    yc = zc[:, :CONV_WIDTH] * conv
    ybuf[:, ATTN_WIDTH:] = (yc * _rms_scale(yc, CONV_WIDTH) * gc_ref[...]).astype(jnp.bfloat16)

    lane = lax.broadcasted_iota(jnp.int32, (WINDOW, LANES), 1)
    lo = lane < HEAD_DIM
    qi = lax.broadcasted_iota(jnp.int32, (WINDOW, 2 * WINDOW), 0)
    kj = lax.broadcasted_iota(jnp.int32, (WINDOW, 2 * WINDOW), 1)
    band = (kj > qi) & (kj <= qi + WINDOW)
    band_first = band & (kj >= jnp.where(first, WINDOW, 0))
    band4 = jnp.concatenate([band] * 4, axis=0)
    band4_first = jnp.concatenate([band_first] * 4, axis=0)

    def sink_rows(heads):
        return jnp.concatenate(
            [jnp.full((WINDOW, LANES), sinks_ref[hd] * LOG2E, jnp.float32) for hd in heads], axis=0)

    sink_a = sink_rows(GROUP_A_HEADS)
    sink_b = sink_rows(GROUP_B_HEADS)
    ones = jnp.ones((2 * WINDOW, LANES), jnp.bfloat16)

    def scores(qg, r0, kk, sink, mask):
        qs = jnp.concatenate(
            [qg[r0:r0 + WINDOW, sl * LANES:(sl + 1) * LANES] for sl in range(4)], axis=0)
        s = lax.dot_general(qs, kk, (((1,), (1,)), ((), ())),
                            preferred_element_type=jnp.float32)
        s = jnp.where(mask, s, NEG_INF)
        m = jnp.max(jnp.maximum(s[:, :LANES], s[:, LANES:]), axis=-1, keepdims=True)
        m = jnp.maximum(jnp.broadcast_to(m, sink.shape), sink)
        p = jnp.exp2(s - jnp.concatenate([m, m], axis=1))
        return p.astype(jnp.bfloat16), m

    def weighted(p, m, vv, sink):
        ov = _dot(p, jnp.concatenate([vv, ones], axis=1))
        l = ov[:, LANES:] + jnp.exp2(sink - m)
        return ov[:, :LANES] / l

    def kv_window(j):
        if j == 0:
            return jnp.concatenate([kv_prev, kv[0:WINDOW]], axis=0)
        return kv[(j - 1) * WINDOW:(j + 1) * WINDOW]

    def block_scores(j):
        kvw = kv_window(j)
        mask = band4_first if j == 0 else band4
        return (scores(qa, j * WINDOW, kvw[:, 0:LANES], sink_a, mask),
                scores(qb, j * WINDOW, kvw[:, LANES:2 * LANES], sink_b, mask))

    def block_output(j, pm):
        kvw = kv_window(j)
        (p_a, m_a), (p_b, m_b) = pm
        o_a = weighted(p_a, m_a, kvw[:, 2 * LANES:3 * LANES], sink_a)
        o_b = weighted(p_b, m_b, kvw[:, 3 * LANES:4 * LANES], sink_b)
        rows = lambda o, sl: o[sl * WINDOW:(sl + 1) * WINDOW]
        y = jnp.concatenate([jnp.where(lo, rows(o_a, 0), rows(o_b, 0)),
                             jnp.where(lo, rows(o_a, 1), rows(o_b, 1)),
                             jnp.where(lo, rows(o_b, 2), rows(o_a, 2)),
                             jnp.where(lo, rows(o_b, 3), rows(o_a, 3))], axis=1)
        y = y * _rms_scale(y, ATTN_WIDTH) * ga_ref[...]
        ybuf[j * WINDOW:(j + 1) * WINDOW, 0:ATTN_WIDTH] = y.astype(jnp.bfloat16)

    n_blocks = tm // WINDOW
    mlp_pieces(MLP_PIECES_BEFORE_ATTENTION)
    pending = block_scores(0)
    for j in range(n_blocks):
        upcoming = block_scores(j + 1) if j + 1 < n_blocks else None
        mlp_pieces(MLP_PIECES_PER_BLOCK)
        block_output(j, pending)
        pending = upcoming
    x1 = x + _dot(ybuf[:, ATTN_WIDTH:], wo_ref[ATTN_WIDTH:, :])
    mlp_pieces(MLP_PIECES_BEFORE_OUT_PROJ)
    x1 = x1 + _dot(ybuf[:, :ATTN_WIDTH], wo_ref[:ATTN_WIDTH, :])
    mlp_pieces(len(pieces))

    x1_prev[...] = x1
    h_prev[...] = (x1 * _rms_scale(x1, D_MODEL) * g2_ref[...]).astype(jnp.bfloat16)


def _layer(x2, sinks, g1, w_in, gq, gk, conv_w, gc, ga, w_out, g2, w_up, w_down, seq_len):
    t = x2.shape[0]
    tm = TOKEN_TILE
    n_tiles = t // tm
    const = lambda shape: pl.BlockSpec(
        shape, lambda i, *_: (0,) * len(shape), pipeline_mode=pl.Buffered(1))
    grid_spec = pltpu.PrefetchScalarGridSpec(
        num_scalar_prefetch=1,
        grid=(n_tiles + 1,),
        in_specs=[
            pl.BlockSpec((tm, D_MODEL), lambda i, *_: (jnp.minimum(i, n_tiles - 1), 0)),
            const((1, D_MODEL)), const((1, HEAD_DIM)), const((1, HEAD_DIM)),
            const((CONV_K, CONV_WIDTH)), const((1, CONV_WIDTH)), const((1, ATTN_WIDTH)),
            const((1, D_MODEL)),
        ] + [pl.BlockSpec(memory_space=pl.ANY)] * 4,
        out_specs=pl.BlockSpec((tm, D_MODEL), lambda i, *_: (jnp.maximum(i - 1, 0), 0)),
        scratch_shapes=[
            pltpu.VMEM((tm, D_MODEL), jnp.float32),
            pltpu.VMEM((tm, D_MODEL), jnp.bfloat16),
            pltpu.VMEM((WINDOW, 4 * LANES), jnp.bfloat16),
            pltpu.VMEM((SUBLANES, CONV_WIDTH), jnp.float32),
            pltpu.VMEM((tm, D_MODEL), jnp.bfloat16),
            pltpu.VMEM((D_MODEL, IN_COLS), jnp.bfloat16),
            pltpu.VMEM((D_MODEL, D_MODEL), jnp.bfloat16),
            pltpu.VMEM((D_MODEL, D_FF), jnp.bfloat16),
            pltpu.VMEM((D_FF, D_MODEL), jnp.bfloat16),
            pltpu.VMEM((2,) + WEIGHT_STAGE_BLOCK, jnp.float32),
            pltpu.SemaphoreType.DMA((2,)),
        ],
    )
    return pl.pallas_call(
        functools.partial(_layer_kernel, seq_len // tm, n_tiles),
        out_shape=jax.ShapeDtypeStruct((t, D_MODEL), jnp.float32),
        grid_spec=grid_spec,
        compiler_params=pltpu.CompilerParams(
            dimension_semantics=("arbitrary",), vmem_limit_bytes=VMEM_LIMIT_BYTES),
        name="layer",
    )(sinks, x2, g1, gq, gk, conv_w, gc, ga, g2, w_in, w_out, w_up, w_down)


def kernel(x, attn_norm_g, w_in, q_norm_g, k_norm_g, sinks, conv_w, attn_out_g, conv_out_g,
           w_out, mlp_norm_g, w_up, w_down):
    b, s, d = x.shape
    depth = w_in.shape[0]
    assert d == D_MODEL and s % TOKEN_TILE == 0
    x2 = x.reshape(b * s, d)
    for l in range(depth):
        row = lambda p: p[l:l + 1]
        x2 = _layer(x2, sinks[l], row(attn_norm_g), w_in[l], row(q_norm_g), row(k_norm_g),
                    conv_w[l], row(conv_out_g), row(attn_out_g), w_out[l], row(mlp_norm_g),
                    w_up[l], w_down[l], s)
    return x2.reshape(b, s, d)
```

```python
import functools
import math

import jax
import jax.numpy as jnp
from jax import lax
from jax.experimental import pallas as pl
from jax.experimental.pallas import tpu as pltpu

D_MODEL = 1024
N_HEADS = 8
HEAD_DIM = 64
N_KV_HEADS = 2
ATTN_WIDTH = N_HEADS * HEAD_DIM
KV_WIDTH = N_KV_HEADS * HEAD_DIM
CONV_WIDTH = D_MODEL - ATTN_WIDTH
CONV_K = 3
WINDOW = 128
IN_COLS = ATTN_WIDTH + 2 * KV_WIDTH + 3 * CONV_WIDTH
D_FF = 4 * D_MODEL
EPS = 1e-6
NEG_INF = -1e30
LOG2E = math.log2(math.e)

LANES = 128
SUBLANES = 8
TOKEN_TILE = 512
FF_CHUNK = 1024
MXU_COLS = 256
MLP_PIECES_BEFORE_ATTENTION = 2
MLP_PIECES_PER_BLOCK = 2
MLP_PIECES_BEFORE_OUT_PROJ = 2
WEIGHT_STAGE_BLOCK = (512, 1024)
WEIGHT_STAGE_SLOTS = 4
VMEM_LIMIT_BYTES = 56 * 1024 * 1024

GROUP_A_HEADS = (0, 2, 5, 7)
GROUP_B_HEADS = (1, 3, 4, 6)


def _rms_scale(xf, width):
    return lax.rsqrt(jnp.sum(xf * xf, axis=-1, keepdims=True) * (1.0 / width) + EPS)


def _dot(a, b):
    return jnp.dot(a, b, preferred_element_type=jnp.float32)


def _head_mean_square(z):
    lo = lax.broadcasted_iota(jnp.int32, (z.shape[0], LANES), 1) < HEAD_DIM
    out = []
    for s in range(z.shape[1] // LANES):
        sq = z[:, s * LANES:(s + 1) * LANES]
        sq = sq * sq
        lo_sum = jnp.sum(jnp.where(lo, sq, 0.0), axis=-1, keepdims=True)
        hi_sum = jnp.sum(jnp.where(lo, 0.0, sq), axis=-1, keepdims=True)
        out.append(jnp.where(lo, lo_sum, hi_sum) * (1.0 / HEAD_DIM))
    return jnp.concatenate(out, axis=1) if len(out) > 1 else out[0]


def _weight_jobs(hbm_refs, vmem_refs, stage, sem):
    slots, rows, cols = stage.shape
    jobs = [(src, dst, r0, c0, min(cols, src.shape[1] - c0))
            for src, dst in zip(hbm_refs, vmem_refs)
            for r0 in range(0, src.shape[0], rows) for c0 in range(0, src.shape[1], cols)]

    def copy(j):
        src, _, r0, c0, nc = jobs[j]
        return pltpu.make_async_copy(src.at[pl.ds(r0, rows), pl.ds(c0, nc)],
                                     stage.at[j % slots, :, pl.ds(0, nc)], sem.at[j % slots])

    def run(j):
        _, dst, r0, c0, nc = jobs[j]
        copy(j).wait()
        dst[r0:r0 + rows, c0:c0 + nc] = stage[j % slots, :, 0:nc].astype(jnp.bfloat16)
        if j + slots < len(jobs):
            copy(j + slots).start()

    for j in range(min(slots, len(jobs))):
        copy(j).start()
    return [functools.partial(run, j) for j in range(len(jobs))]


def _layer_kernel(tiles_per_seq, n_tiles, sinks_ref, x_ref, g1_ref, gq_ref, gk_ref, cw_ref, gc_ref,
                  ga_ref, g2_ref, win_hbm, wo_hbm, wu_hbm, wd_hbm, o_ref,
                  x1_prev, h_prev, kv_tail, u_tail, ybuf, win_ref, wo_ref, wu_ref, wd_ref,
                  stage, sem):
    step = pl.program_id(0)
    refs = (sinks_ref, x_ref, g1_ref, win_ref, gq_ref, gk_ref, cw_ref, gc_ref,
            ga_ref, wo_ref, g2_ref, wu_ref, wd_ref, o_ref, x1_prev, h_prev, kv_tail, u_tail, ybuf)

    @pl.when(step == 0)
    def _():
        kv_tail[...] = jnp.zeros_like(kv_tail)
        u_tail[...] = jnp.zeros_like(u_tail)
        for job in _weight_jobs((win_hbm, wo_hbm), (win_ref, wo_ref), stage, sem):
            job()
        _step(True, _weight_jobs((wu_hbm, wd_hbm), (wu_ref, wd_ref), stage, sem), True, *refs)

    @pl.when(jnp.logical_and(step > 0, step < n_tiles))
    def _():
        _step(True, None, (step % tiles_per_seq) == 0, *refs)

    @pl.when(step == n_tiles)
    def _():
        _step(False, None, False, *refs)


def _step(with_mixer, fillers, first, sinks_ref, x_ref, g1_ref, win_ref, gq_ref, gk_ref,
          cw_ref, gc_ref, ga_ref, wo_ref, g2_ref, wu_ref, wd_ref,
          o_ref, x1_prev, h_prev, kv_tail, u_tail, ybuf):
    tm = TOKEN_TILE
    with_mlp = fillers is None

    if with_mlp:
        o_ref[...] = x1_prev[...]
    n_up = FF_CHUNK // MXU_COLS
    n_down = D_MODEL // MXU_COLS
    acts = {}

    def up_piece(c, n):
        w0 = c * FF_CHUNK + n * MXU_COLS
        a = jnp.maximum(_dot(h_prev[...], wu_ref[:, w0:w0 + MXU_COLS]), 0.0)
        acts[c, n] = (a * a).astype(jnp.bfloat16)

    def down_piece(c, n):
        a = jnp.concatenate([acts[c, i] for i in range(n_up)], axis=1)
        o_ref[:, n * MXU_COLS:(n + 1) * MXU_COLS] += _dot(
            a, wd_ref[c * FF_CHUNK:(c + 1) * FF_CHUNK, n * MXU_COLS:(n + 1) * MXU_COLS])

    pieces = list(fillers) if fillers is not None else [
        functools.partial(f, c, n) for c in range(D_FF // FF_CHUNK)
        for f, count in ((up_piece, n_up), (down_piece, n_down)) for n in range(count)]

    def mlp_pieces(count):
        for _ in range(min(count, len(pieces))):
            pieces.pop(0)()

    if not with_mixer:
        mlp_pieces(len(pieces))
        return
    mlp_pieces(4)

    x = x_ref[...]
    h = (x * _rms_scale(x, D_MODEL) * g1_ref[...]).astype(jnp.bfloat16)
    o0 = ATTN_WIDTH
    o1 = o0 + 2 * KV_WIDTH
    zq = _dot(h, win_ref[:, :o0])
    zkv = _dot(h, win_ref[:, o0:o1])
    k = zkv[:, :KV_WIDTH]
    v = zkv[:, KV_WIDTH:]
    half = ATTN_WIDTH // 2
    q_ms = _head_mean_square(zq)
    k_ms = _head_mean_square(k)
    zc = _dot(h, win_ref[:, o1:])
    gq = jnp.tile(gq_ref[...], (1, N_HEADS)) * (HEAD_DIM ** -0.5 * LOG2E)
    gk = jnp.tile(gk_ref[...], (1, N_KV_HEADS))
    qn = zq * lax.rsqrt(q_ms + EPS) * gq
    kn = k * lax.rsqrt(k_ms + EPS) * gk
    col = lax.broadcasted_iota(jnp.int32, (tm, ATTN_WIDTH), 1)
    in_a = ((col % LANES) < HEAD_DIM) == (col < half)
    qa = jnp.where(in_a, qn, 0.0).astype(jnp.bfloat16)
    qb = jnp.where(in_a, 0.0, qn).astype(jnp.bfloat16)
    bf = lambda t: t.astype(jnp.bfloat16)
    kv = jnp.concatenate([bf(kn), bf(pltpu.roll(kn, HEAD_DIM, 1)),
                          bf(v), bf(pltpu.roll(v, HEAD_DIM, 1))], axis=1)
    kv_prev = kv_tail[...]
    kv_tail[...] = kv[tm - WINDOW:]

    u = zc[:, CONV_WIDTH:2 * CONV_WIDTH] * zc[:, 2 * CONV_WIDTH:]
    ext = jnp.concatenate([jnp.where(first, 0.0, u_tail[...]), u], axis=0)
    u1 = pltpu.roll(ext, 1, 0)[SUBLANES:]
    u2 = pltpu.roll(ext, 2, 0)[SUBLANES:]
    u_tail[...] = u[tm - SUBLANES:]
    conv = cw_ref[0:1, :] * u2 + cw_ref[1:2, :] * u1 + cw_ref[2:3, :] * u
    yc = zc[:, :CONV_WIDTH] * conv
    ybuf[:, ATTN_WIDTH:] = (yc * _rms_scale(yc, CONV_WIDTH) * gc_ref[...]).astype(jnp.bfloat16)

    lane = lax.broadcasted_iota(jnp.int32, (WINDOW, LANES), 1)
    lo = lane < HEAD_DIM
    qi = lax.broadcasted_iota(jnp.int32, (WINDOW, 2 * WINDOW), 0)
    kj = lax.broadcasted_iota(jnp.int32, (WINDOW, 2 * WINDOW), 1)
    band = (kj > qi) & (kj <= qi + WINDOW)
    band_first = band & (kj >= jnp.where(first, WINDOW, 0))
    band4 = jnp.concatenate([band] * 4, axis=0)
    band4_first = jnp.concatenate([band_first] * 4, axis=0)

    def sink_rows(heads):
        return jnp.concatenate(
            [jnp.full((WINDOW, LANES), sinks_ref[hd] * LOG2E, jnp.float32) for hd in heads], axis=0)

    sink_a = sink_rows(GROUP_A_HEADS)
    sink_b = sink_rows(GROUP_B_HEADS)
    ones = jnp.ones((2 * WINDOW, LANES), jnp.bfloat16)

    def scores(qg, r0, kk, sink, mask):
        qs = jnp.concatenate(
            [qg[r0:r0 + WINDOW, sl * LANES:(sl + 1) * LANES] for sl in range(4)], axis=0)
        s = lax.dot_general(qs, kk, (((1,), (1,)), ((), ())),
                            preferred_element_type=jnp.float32)
        s = jnp.where(mask, s, NEG_INF)
        m = jnp.max(jnp.maximum(s[:, :LANES], s[:, LANES:]), axis=-1, keepdims=True)
        m = jnp.maximum(jnp.broadcast_to(m, sink.shape), sink)
        p = jnp.exp2(s - jnp.concatenate([m, m], axis=1))
        return p.astype(jnp.bfloat16), m

    def weighted(p, m, vv, sink):
        ov = _dot(p, jnp.concatenate([vv, ones], axis=1))
        l = ov[:, LANES:] + jnp.exp2(sink - m)
        return ov[:, :LANES] / l

    def kv_window(j):
        if j == 0:
            return jnp.concatenate([kv_prev, kv[0:WINDOW]], axis=0)
        return kv[(j - 1) * WINDOW:(j + 1) * WINDOW]

    def block_scores(j):
        kvw = kv_window(j)
        mask = band4_first if j == 0 else band4
        return (scores(qa, j * WINDOW, kvw[:, 0:LANES], sink_a, mask),
                scores(qb, j * WINDOW, kvw[:, LANES:2 * LANES], sink_b, mask))

    def block_output(j, pm):
        kvw = kv_window(j)
        (p_a, m_a), (p_b, m_b) = pm
        o_a = weighted(p_a, m_a, kvw[:, 2 * LANES:3 * LANES], sink_a)
        o_b = weighted(p_b, m_b, kvw[:, 3 * LANES:4 * LANES], sink_b)
        rows = lambda o, sl: o[sl * WINDOW:(sl + 1) * WINDOW]
        y = jnp.concatenate([jnp.where(lo, rows(o_a, 0), rows(o_b, 0)),
                             jnp.where(lo, rows(o_a, 1), rows(o_b, 1)),
                             jnp.where(lo, rows(o_b, 2), rows(o_a, 2)),
                             jnp.where(lo, rows(o_b, 3), rows(o_a, 3))], axis=1)
        y = y * _rms_scale(y, ATTN_WIDTH) * ga_ref[...]
        ybuf[j * WINDOW:(j + 1) * WINDOW, 0:ATTN_WIDTH] = y.astype(jnp.bfloat16)

    n_blocks = tm // WINDOW
    mlp_pieces(MLP_PIECES_BEFORE_ATTENTION)
    pending = block_scores(0)
    for j in range(n_blocks):
        upcoming = block_scores(j + 1) if j + 1 < n_blocks else None
        mlp_pieces(MLP_PIECES_PER_BLOCK)
        block_output(j, pending)
        pending = upcoming
    x1 = x + _dot(ybuf[:, ATTN_WIDTH:], wo_ref[ATTN_WIDTH:, :])
    mlp_pieces(MLP_PIECES_BEFORE_OUT_PROJ)
    x1 = x1 + _dot(ybuf[:, :ATTN_WIDTH], wo_ref[:ATTN_WIDTH, :])
    mlp_pieces(len(pieces))

    x1_prev[...] = x1
    h_prev[...] = (x1 * _rms_scale(x1, D_MODEL) * g2_ref[...]).astype(jnp.bfloat16)


def _layer(x2, sinks, g1, w_in, gq, gk, conv_w, gc, ga, w_out, g2, w_up, w_down, seq_len):
    t = x2.shape[0]
    tm = TOKEN_TILE
    n_tiles = t // tm
    const = lambda shape: pl.BlockSpec(
        shape, lambda i, *_: (0,) * len(shape), pipeline_mode=pl.Buffered(1))
    grid_spec = pltpu.PrefetchScalarGridSpec(
        num_scalar_prefetch=1,
        grid=(n_tiles + 1,),
        in_specs=[
            pl.BlockSpec((tm, D_MODEL), lambda i, *_: (jnp.minimum(i, n_tiles - 1), 0)),
            const((1, D_MODEL)), const((1, HEAD_DIM)), const((1, HEAD_DIM)),
            pl.BlockSpec((None, CONV_K, CONV_WIDTH), lambda i, *_: (0, 0, 0),
                         pipeline_mode=pl.Buffered(1)),
            const((1, CONV_WIDTH)), const((1, ATTN_WIDTH)), const((1, D_MODEL)),
        ] + [pl.BlockSpec(memory_space=pl.ANY)] * 4,
        out_specs=pl.BlockSpec((tm, D_MODEL), lambda i, *_: (jnp.maximum(i - 1, 0), 0)),
        scratch_shapes=[
            pltpu.VMEM((tm, D_MODEL), jnp.float32),
            pltpu.VMEM((tm, D_MODEL), jnp.bfloat16),
            pltpu.VMEM((WINDOW, 4 * LANES), jnp.bfloat16),
            pltpu.VMEM((SUBLANES, CONV_WIDTH), jnp.float32),
            pltpu.VMEM((tm, D_MODEL), jnp.bfloat16),
            pltpu.VMEM((D_MODEL, IN_COLS), jnp.bfloat16),
            pltpu.VMEM((D_MODEL, D_MODEL), jnp.bfloat16),
            pltpu.VMEM((D_MODEL, D_FF), jnp.bfloat16),
            pltpu.VMEM((D_FF, D_MODEL), jnp.bfloat16),
            pltpu.VMEM((WEIGHT_STAGE_SLOTS,) + WEIGHT_STAGE_BLOCK, jnp.float32),
            pltpu.SemaphoreType.DMA((WEIGHT_STAGE_SLOTS,)),
        ],
    )
    return pl.pallas_call(
        functools.partial(_layer_kernel, seq_len // tm, n_tiles),
        out_shape=jax.ShapeDtypeStruct((t, D_MODEL), jnp.float32),
        grid_spec=grid_spec,
        compiler_params=pltpu.CompilerParams(
            dimension_semantics=("arbitrary",), vmem_limit_bytes=VMEM_LIMIT_BYTES),
        name="layer",
    )(sinks, x2, g1, gq, gk, conv_w, gc, ga, g2, w_in, w_out, w_up, w_down)


def kernel(x, attn_norm_g, w_in, q_norm_g, k_norm_g, sinks, conv_w, attn_out_g, conv_out_g,
           w_out, mlp_norm_g, w_up, w_down):
    b, s, d = x.shape
    depth = w_in.shape[0]
    assert d == D_MODEL and s % TOKEN_TILE == 0
    x2 = x.reshape(b * s, d)
    for l in range(depth):
        row = lambda p: p[l:l + 1]
        x2 = _layer(x2, sinks[l], row(attn_norm_g), w_in[l], row(q_norm_g), row(k_norm_g),
                    row(conv_w), row(conv_out_g), row(attn_out_g), w_out[l], row(mlp_norm_g),
                    w_up[l], w_down[l], s)
    return x2.reshape(b, s, d)
```

```python
import functools
import math

import jax
import jax.numpy as jnp
from jax import lax
from jax.experimental import pallas as pl
from jax.experimental.pallas import tpu as pltpu

D_MODEL = 1024
N_HEADS = 8
HEAD_DIM = 64
N_KV_HEADS = 2
ATTN_WIDTH = N_HEADS * HEAD_DIM
KV_WIDTH = N_KV_HEADS * HEAD_DIM
CONV_WIDTH = D_MODEL - ATTN_WIDTH
CONV_K = 3
WINDOW = 128
IN_COLS = ATTN_WIDTH + 2 * KV_WIDTH + 3 * CONV_WIDTH
D_FF = 4 * D_MODEL
EPS = 1e-6
NEG_INF = -1e30
LOG2E = math.log2(math.e)

LANES = 128
SUBLANES = 8
TOKEN_TILE = 512
FF_CHUNK = 1024
MXU_COLS = 256
MLP_PIECES_BEFORE_MIXER = 4
MLP_PIECES_BEFORE_ATTENTION = 2
MLP_PIECES_PER_BLOCK = 2
MLP_PIECES_BEFORE_OUT_PROJ = 2
WEIGHT_STAGE_BLOCK = (512, 1024)
WEIGHT_STAGE_SLOTS = 4
V7X_VMEM_BYTES = 64 * 1024 * 1024
VMEM_LIMIT_BYTES = V7X_VMEM_BYTES // 8 * 7

GROUP_A_HEADS = (0, 2, 5, 7)
GROUP_B_HEADS = (1, 3, 4, 6)


def _rms_scale(xf, width):
    return lax.rsqrt(jnp.sum(xf * xf, axis=-1, keepdims=True) * (1.0 / width) + EPS)


def _dot(a, b):
    return jnp.dot(a, b, preferred_element_type=jnp.float32)


def _head_mean_square(z):
    lo = lax.broadcasted_iota(jnp.int32, (z.shape[0], LANES), 1) < HEAD_DIM
    out = []
    for s in range(z.shape[1] // LANES):
        sq = z[:, s * LANES:(s + 1) * LANES]
        sq = sq * sq
        lo_sum = jnp.sum(jnp.where(lo, sq, 0.0), axis=-1, keepdims=True)
        hi_sum = jnp.sum(jnp.where(lo, 0.0, sq), axis=-1, keepdims=True)
        out.append(jnp.where(lo, lo_sum, hi_sum) * (1.0 / HEAD_DIM))
    return jnp.concatenate(out, axis=1) if len(out) > 1 else out[0]


def _weight_jobs(hbm_refs, vmem_refs, stage, sem):
    slots, rows, cols = stage.shape
    assert all(src.shape[0] % rows == 0 and src.shape == dst.shape
               for src, dst in zip(hbm_refs, vmem_refs))
    jobs = [(src, dst, r0, c0, min(cols, src.shape[1] - c0))
            for src, dst in zip(hbm_refs, vmem_refs)
            for r0 in range(0, src.shape[0], rows) for c0 in range(0, src.shape[1], cols)]

    def copy(j):
        src, _, r0, c0, nc = jobs[j]
        return pltpu.make_async_copy(src.at[pl.ds(r0, rows), pl.ds(c0, nc)],
                                     stage.at[j % slots, :, pl.ds(0, nc)], sem.at[j % slots])

    def run(j):
        _, dst, r0, c0, nc = jobs[j]
        copy(j).wait()
        dst[r0:r0 + rows, c0:c0 + nc] = stage[j % slots, :, 0:nc].astype(jnp.bfloat16)
        if j + slots < len(jobs):
            copy(j + slots).start()

    for j in range(min(slots, len(jobs))):
        copy(j).start()
    return [functools.partial(run, j) for j in range(len(jobs))]


def _layer_kernel(tiles_per_seq, n_tiles, sinks_ref, x_ref, g1_ref, gq_ref, gk_ref, cw_ref, gc_ref,
                  ga_ref, g2_ref, win_hbm, wo_hbm, wu_hbm, wd_hbm, o_ref,
                  x1_prev, h_prev, kv_tail, u_tail, ybuf, win_ref, wo_ref, wu_ref, wd_ref,
                  stage, sem):
    step = pl.program_id(0)
    refs = (sinks_ref, x_ref, g1_ref, win_ref, gq_ref, gk_ref, cw_ref, gc_ref,
            ga_ref, wo_ref, g2_ref, wu_ref, wd_ref, o_ref, x1_prev, h_prev, kv_tail, u_tail, ybuf)

    @pl.when(step == 0)
    def _():
        kv_tail[...] = jnp.zeros_like(kv_tail)
        u_tail[...] = jnp.zeros_like(u_tail)
        for job in _weight_jobs((win_hbm,), (win_ref,), stage, sem):
            job()
        wo_blocks = (wo_hbm.shape[0] // stage.shape[1]) * pl.cdiv(wo_hbm.shape[1], stage.shape[2])
        assert wo_blocks <= MLP_PIECES_BEFORE_MIXER
        _step(True, _weight_jobs((wo_hbm, wu_hbm, wd_hbm), (wo_ref, wu_ref, wd_ref), stage, sem),
              True, *refs)

    @pl.when(jnp.logical_and(step > 0, step < n_tiles))
    def _():
        _step(True, None, (step % tiles_per_seq) == 0, *refs)

    @pl.when(step == n_tiles)
    def _():
        _step(False, None, False, *refs)


def _step(with_mixer, fillers, first, sinks_ref, x_ref, g1_ref, win_ref, gq_ref, gk_ref,
          cw_ref, gc_ref, ga_ref, wo_ref, g2_ref, wu_ref, wd_ref,
          o_ref, x1_prev, h_prev, kv_tail, u_tail, ybuf):
    tm = TOKEN_TILE
    with_mlp = fillers is None

    if with_mlp:
        o_ref[...] = x1_prev[...]
    n_up = FF_CHUNK // MXU_COLS
    n_down = D_MODEL // MXU_COLS
    acts = {}

    def up_piece(c, n):
        w0 = c * FF_CHUNK + n * MXU_COLS
        a = jnp.maximum(_dot(h_prev[...], wu_ref[:, w0:w0 + MXU_COLS]), 0.0)
        acts[c, n] = (a * a).astype(jnp.bfloat16)

    def down_piece(c, n):
        a = jnp.concatenate([acts[c, i] for i in range(n_up)], axis=1)
        o_ref[:, n * MXU_COLS:(n + 1) * MXU_COLS] += _dot(
            a, wd_ref[c * FF_CHUNK:(c + 1) * FF_CHUNK, n * MXU_COLS:(n + 1) * MXU_COLS])

    pieces = list(fillers) if fillers is not None else [
        functools.partial(f, c, n) for c in range(D_FF // FF_CHUNK)
        for f, count in ((up_piece, n_up), (down_piece, n_down)) for n in range(count)]

    def mlp_pieces(count):
        for _ in range(min(count, len(pieces))):
            pieces.pop(0)()

    if not with_mixer:
        mlp_pieces(len(pieces))
        return
    mlp_pieces(MLP_PIECES_BEFORE_MIXER)

    x = x_ref[...]
    h = (x * _rms_scale(x, D_MODEL) * g1_ref[...]).astype(jnp.bfloat16)
    o0 = ATTN_WIDTH
    o1 = o0 + 2 * KV_WIDTH
    zq = _dot(h, win_ref[:, :o0])
    zkv = _dot(h, win_ref[:, o0:o1])
    k = zkv[:, :KV_WIDTH]
    v = zkv[:, KV_WIDTH:]
    half = ATTN_WIDTH // 2
    q_ms = _head_mean_square(zq)
    k_ms = _head_mean_square(k)
    zc = _dot(h, win_ref[:, o1:])
    gq = jnp.tile(gq_ref[...], (1, N_HEADS)) * (HEAD_DIM ** -0.5 * LOG2E)
    gk = jnp.tile(gk_ref[...], (1, N_KV_HEADS))
    qn = zq * lax.rsqrt(q_ms + EPS) * gq
    kn = k * lax.rsqrt(k_ms + EPS) * gk
    col = lax.broadcasted_iota(jnp.int32, (tm, ATTN_WIDTH), 1)
    in_a = ((col % LANES) < HEAD_DIM) == (col < half)
    qa = jnp.where(in_a, qn, 0.0).astype(jnp.bfloat16)
    qb = jnp.where(in_a, 0.0, qn).astype(jnp.bfloat16)
    bf = lambda t: t.astype(jnp.bfloat16)
    kv = jnp.concatenate([bf(kn), bf(pltpu.roll(kn, HEAD_DIM, 1)),
                          bf(v), bf(pltpu.roll(v, HEAD_DIM, 1))], axis=1)
    kv_prev = kv_tail[...]
    kv_tail[...] = kv[tm - WINDOW:]

    u = zc[:, CONV_WIDTH:2 * CONV_WIDTH] * zc[:, 2 * CONV_WIDTH:]
    ext = jnp.concatenate([jnp.where(first, 0.0, u_tail[...]), u], axis=0)
    u1 = pltpu.roll(ext, 1, 0)[SUBLANES:]
    u2 = pltpu.roll(ext, 2, 0)[SUBLANES:]
    u_tail[...] = u[tm - SUBLANES:]
    conv = cw_ref[0:1, :] * u2 + cw_ref[1:2, :] * u1 + cw_ref[2:3, :] * u
    yc = zc[:, :CONV_WIDTH] * conv
    ybuf[:, ATTN_WIDTH:] = (yc * _rms_scale(yc, CONV_WIDTH) * gc_ref[...]).astype(jnp.bfloat16)

    lane = lax.broadcasted_iota(jnp.int32, (WINDOW, LANES), 1)
    lo = lane < HEAD_DIM
    qi = lax.broadcasted_iota(jnp.int32, (WINDOW, 2 * WINDOW), 0)
    kj = lax.broadcasted_iota(jnp.int32, (WINDOW, 2 * WINDOW), 1)
    band = (kj > qi) & (kj <= qi + WINDOW)
    band_first = band & (kj >= jnp.where(first, WINDOW, 0))
    band4 = jnp.concatenate([band] * 4, axis=0)
    band4_first = jnp.concatenate([band_first] * 4, axis=0)

    def sink_rows(heads):
        return jnp.concatenate(
            [jnp.full((WINDOW, LANES), sinks_ref[hd] * LOG2E, jnp.float32) for hd in heads], axis=0)

    sink_a = sink_rows(GROUP_A_HEADS)
    sink_b = sink_rows(GROUP_B_HEADS)
    ones = jnp.ones((2 * WINDOW, LANES), jnp.bfloat16)

    def scores(qg, r0, kk, sink, mask):
        qs = jnp.concatenate(
            [qg[r0:r0 + WINDOW, sl * LANES:(sl + 1) * LANES] for sl in range(4)], axis=0)
        s = lax.dot_general(qs, kk, (((1,), (1,)), ((), ())),
                            preferred_element_type=jnp.float32)
        s = jnp.where(mask, s, NEG_INF)
        m = jnp.max(jnp.maximum(s[:, :LANES], s[:, LANES:]), axis=-1, keepdims=True)
        m = jnp.maximum(jnp.broadcast_to(m, sink.shape), sink)
        p = jnp.exp2(s - jnp.concatenate([m, m], axis=1))
        return p.astype(jnp.bfloat16), m

    def weighted(p, m, vv, sink):
        ov = _dot(p, jnp.concatenate([vv, ones], axis=1))
        l = ov[:, LANES:] + jnp.exp2(sink - m)
        return ov[:, :LANES] / l

    def kv_window(j):
        if j == 0:
            return jnp.concatenate([kv_prev, kv[0:WINDOW]], axis=0)
        return kv[(j - 1) * WINDOW:(j + 1) * WINDOW]

    def block_scores(j):
        kvw = kv_window(j)
        mask = band4_first if j == 0 else band4
        return (scores(qa, j * WINDOW, kvw[:, 0:LANES], sink_a, mask),
                scores(qb, j * WINDOW, kvw[:, LANES:2 * LANES], sink_b, mask))

    def block_output(j, pm):
        kvw = kv_window(j)
        (p_a, m_a), (p_b, m_b) = pm
        o_a = weighted(p_a, m_a, kvw[:, 2 * LANES:3 * LANES], sink_a)
        o_b = weighted(p_b, m_b, kvw[:, 3 * LANES:4 * LANES], sink_b)
        rows = lambda o, sl: o[sl * WINDOW:(sl + 1) * WINDOW]
        y = jnp.concatenate([jnp.where(lo, rows(o_a, 0), rows(o_b, 0)),
                             jnp.where(lo, rows(o_a, 1), rows(o_b, 1)),
                             jnp.where(lo, rows(o_b, 2), rows(o_a, 2)),
                             jnp.where(lo, rows(o_b, 3), rows(o_a, 3))], axis=1)
        y = y * _rms_scale(y, ATTN_WIDTH) * ga_ref[...]
        ybuf[j * WINDOW:(j + 1) * WINDOW, 0:ATTN_WIDTH] = y.astype(jnp.bfloat16)

    n_blocks = tm // WINDOW
    mlp_pieces(MLP_PIECES_BEFORE_ATTENTION)
    pending = block_scores(0)
    for j in range(n_blocks):
        upcoming = block_scores(j + 1) if j + 1 < n_blocks else None
        mlp_pieces(MLP_PIECES_PER_BLOCK)
        block_output(j, pending)
        pending = upcoming
    x1 = x + _dot(ybuf[:, ATTN_WIDTH:], wo_ref[ATTN_WIDTH:, :])
    mlp_pieces(MLP_PIECES_BEFORE_OUT_PROJ)
    x1 = x1 + _dot(ybuf[:, :ATTN_WIDTH], wo_ref[:ATTN_WIDTH, :])
    mlp_pieces(len(pieces))

    x1_prev[...] = x1
    h_prev[...] = (x1 * _rms_scale(x1, D_MODEL) * g2_ref[...]).astype(jnp.bfloat16)


def _layer(x2, sinks, g1, w_in, gq, gk, conv_w, gc, ga, w_out, g2, w_up, w_down, seq_len):
    t = x2.shape[0]
    tm = TOKEN_TILE
    n_tiles = t // tm
    const = lambda shape: pl.BlockSpec(
        shape, lambda i, *_: (0,) * len(shape), pipeline_mode=pl.Buffered(1))
    grid_spec = pltpu.PrefetchScalarGridSpec(
        num_scalar_prefetch=1,
        grid=(n_tiles + 1,),
        in_specs=[
            pl.BlockSpec((tm, D_MODEL), lambda i, *_: (jnp.minimum(i, n_tiles - 1), 0)),
            const((1, D_MODEL)), const((1, HEAD_DIM)), const((1, HEAD_DIM)),
            pl.BlockSpec((None, CONV_K, CONV_WIDTH), lambda i, *_: (0, 0, 0),
                         pipeline_mode=pl.Buffered(1)),
            const((1, CONV_WIDTH)), const((1, ATTN_WIDTH)), const((1, D_MODEL)),
        ] + [pl.BlockSpec(memory_space=pl.ANY)] * 4,
        out_specs=pl.BlockSpec((tm, D_MODEL), lambda i, *_: (jnp.maximum(i - 1, 0), 0)),
        scratch_shapes=[
            pltpu.VMEM((tm, D_MODEL), jnp.float32),
            pltpu.VMEM((tm, D_MODEL), jnp.bfloat16),
            pltpu.VMEM((WINDOW, 4 * LANES), jnp.bfloat16),
            pltpu.VMEM((SUBLANES, CONV_WIDTH), jnp.float32),
            pltpu.VMEM((tm, D_MODEL), jnp.bfloat16),
            pltpu.VMEM((D_MODEL, IN_COLS), jnp.bfloat16),
            pltpu.VMEM((D_MODEL, D_MODEL), jnp.bfloat16),
            pltpu.VMEM((D_MODEL, D_FF), jnp.bfloat16),
            pltpu.VMEM((D_FF, D_MODEL), jnp.bfloat16),
            pltpu.VMEM((WEIGHT_STAGE_SLOTS,) + WEIGHT_STAGE_BLOCK, jnp.float32),
            pltpu.SemaphoreType.DMA((WEIGHT_STAGE_SLOTS,)),
        ],
    )
    return pl.pallas_call(
        functools.partial(_layer_kernel, seq_len // tm, n_tiles),
        out_shape=jax.ShapeDtypeStruct((t, D_MODEL), jnp.float32),
        grid_spec=grid_spec,
        compiler_params=pltpu.CompilerParams(
            dimension_semantics=("arbitrary",), vmem_limit_bytes=VMEM_LIMIT_BYTES),
        name="layer",
    )(sinks, x2, g1, gq, gk, conv_w, gc, ga, g2, w_in, w_out, w_up, w_down)


def kernel(x, attn_norm_g, w_in, q_norm_g, k_norm_g, sinks, conv_w, attn_out_g, conv_out_g,
           w_out, mlp_norm_g, w_up, w_down):
    b, s, d = x.shape
    depth = w_in.shape[0]
    assert d == D_MODEL and s % TOKEN_TILE == 0
    x2 = x.reshape(b * s, d)
    for l in range(depth):
        row = lambda p: p[l:l + 1]
        x2 = _layer(x2, sinks[l], row(attn_norm_g), w_in[l], row(q_norm_g), row(k_norm_g),
                    row(conv_w), row(conv_out_g), row(attn_out_g), w_out[l], row(mlp_norm_g),
                    w_up[l], w_down[l], s)
    return x2.reshape(b, s, d)
```

```python
import functools
import math

import jax
import jax.numpy as jnp
from jax import lax
from jax.experimental import pallas as pl
from jax.experimental.pallas import tpu as pltpu

D_MODEL = 1024
N_HEADS = 8
HEAD_DIM = 64
N_KV_HEADS = 2
ATTN_WIDTH = N_HEADS * HEAD_DIM
KV_WIDTH = N_KV_HEADS * HEAD_DIM
CONV_WIDTH = D_MODEL - ATTN_WIDTH
CONV_K = 3
WINDOW = 128
IN_COLS = ATTN_WIDTH + 2 * KV_WIDTH + 3 * CONV_WIDTH
D_FF = 4 * D_MODEL
EPS = 1e-6
NEG_INF = -1e30
LOG2E = math.log2(math.e)

LANES = 128
SUBLANES = 8
TOKEN_TILE = 512
FF_CHUNK = 1024
MXU_COLS = 256
MLP_PIECES_BEFORE_MIXER = 4
MLP_PIECES_BEFORE_ATTENTION = 2
MLP_PIECES_PER_BLOCK = 2
MLP_PIECES_BEFORE_OUT_PROJ = 2
WEIGHT_STAGE_BLOCK = (512, 1024)
WEIGHT_STAGE_SLOTS = 4
V7X_VMEM_BYTES = 64 * 1024 * 1024
VMEM_LIMIT_BYTES = V7X_VMEM_BYTES // 8 * 7

GROUP_A_HEADS = (0, 2, 5, 7)
GROUP_B_HEADS = (1, 3, 4, 6)


def _rms_scale(xf, width):
    return lax.rsqrt(jnp.sum(xf * xf, axis=-1, keepdims=True) * (1.0 / width) + EPS)


def _dot(a, b):
    return jnp.dot(a, b, preferred_element_type=jnp.float32)


def _head_mean_square(z):
    lo = lax.broadcasted_iota(jnp.int32, (z.shape[0], LANES), 1) < HEAD_DIM
    out = []
    for s in range(z.shape[1] // LANES):
        sq = z[:, s * LANES:(s + 1) * LANES]
        sq = sq * sq
        lo_sum = jnp.sum(jnp.where(lo, sq, 0.0), axis=-1, keepdims=True)
        hi_sum = jnp.sum(jnp.where(lo, 0.0, sq), axis=-1, keepdims=True)
        out.append(jnp.where(lo, lo_sum, hi_sum) * (1.0 / HEAD_DIM))
    return jnp.concatenate(out, axis=1) if len(out) > 1 else out[0]


def _weight_jobs(hbm_refs, vmem_refs, stage, sem):
    slots, rows, cols = stage.shape
    assert all(src.shape[0] % rows == 0 and src.shape == dst.shape
               for src, dst in zip(hbm_refs, vmem_refs))
    jobs = [(src, dst, r0, c0, min(cols, src.shape[1] - c0))
            for src, dst in zip(hbm_refs, vmem_refs)
            for r0 in range(0, src.shape[0], rows) for c0 in range(0, src.shape[1], cols)]

    def copy(j):
        src, _, r0, c0, nc = jobs[j]
        return pltpu.make_async_copy(src.at[pl.ds(r0, rows), pl.ds(c0, nc)],
                                     stage.at[j % slots, :, pl.ds(0, nc)], sem.at[j % slots])

    def run(j):
        _, dst, r0, c0, nc = jobs[j]
        copy(j).wait()
        dst[r0:r0 + rows, c0:c0 + nc] = stage[j % slots, :, 0:nc].astype(jnp.bfloat16)
        if j + slots < len(jobs):
            copy(j + slots).start()

    for j in range(min(slots, len(jobs))):
        copy(j).start()
    return [functools.partial(run, j) for j in range(len(jobs))]


def _layer_kernel(tiles_per_seq, n_tiles, sinks_ref, x_ref, g1_ref, gq_ref, gk_ref, cw_ref, gc_ref,
                  ga_ref, g2_ref, win_hbm, wo_hbm, wu_hbm, wd_hbm, o_ref,
                  x1_prev, h_prev, kv_tail, u_tail, ybuf, win_ref, wo_ref, wu_ref, wd_ref,
                  stage, sem):
    step = pl.program_id(0)
    refs = (sinks_ref, x_ref, g1_ref, win_ref, gq_ref, gk_ref, cw_ref, gc_ref,
            ga_ref, wo_ref, g2_ref, wu_ref, wd_ref, o_ref, x1_prev, h_prev, kv_tail, u_tail, ybuf)

    @pl.when(step == 0)
    def _():
        kv_tail[...] = jnp.zeros_like(kv_tail)
        u_tail[...] = jnp.zeros_like(u_tail)
        for job in _weight_jobs((win_hbm, wo_hbm), (win_ref, wo_ref), stage, sem):
            job()
        _step(True, _weight_jobs((wu_hbm, wd_hbm), (wu_ref, wd_ref), stage, sem), True, *refs)

    @pl.when(jnp.logical_and(step > 0, step < n_tiles))
    def _():
        _step(True, None, (step % tiles_per_seq) == 0, *refs)

    @pl.when(step == n_tiles)
    def _():
        _step(False, None, False, *refs)


def _step(with_mixer, fillers, first, sinks_ref, x_ref, g1_ref, win_ref, gq_ref, gk_ref,
          cw_ref, gc_ref, ga_ref, wo_ref, g2_ref, wu_ref, wd_ref,
          o_ref, x1_prev, h_prev, kv_tail, u_tail, ybuf):
    tm = TOKEN_TILE
    with_mlp = fillers is None

    if with_mlp:
        o_ref[...] = x1_prev[...]
    n_up = FF_CHUNK // MXU_COLS
    n_down = D_MODEL // MXU_COLS
    acts = {}

    def up_piece(c, n):
        w0 = c * FF_CHUNK + n * MXU_COLS
        a = jnp.maximum(_dot(h_prev[...], wu_ref[:, w0:w0 + MXU_COLS]), 0.0)
        acts[c, n] = (a * a).astype(jnp.bfloat16)

    def down_piece(c, n):
        a = jnp.concatenate([acts[c, i] for i in range(n_up)], axis=1)
        o_ref[:, n * MXU_COLS:(n + 1) * MXU_COLS] += _dot(
            a, wd_ref[c * FF_CHUNK:(c + 1) * FF_CHUNK, n * MXU_COLS:(n + 1) * MXU_COLS])

    pieces = list(fillers) if fillers is not None else [
        functools.partial(f, c, n) for c in range(D_FF // FF_CHUNK)
        for f, count in ((up_piece, n_up), (down_piece, n_down)) for n in range(count)]

    def mlp_pieces(count):
        for _ in range(min(count, len(pieces))):
            pieces.pop(0)()

    if not with_mixer:
        mlp_pieces(len(pieces))
        return
    mlp_pieces(MLP_PIECES_BEFORE_MIXER)

    x = x_ref[...]
    h = (x * _rms_scale(x, D_MODEL) * g1_ref[...]).astype(jnp.bfloat16)
    o0 = ATTN_WIDTH
    o1 = o0 + 2 * KV_WIDTH
    zq = _dot(h, win_ref[:, :o0])
    zkv = _dot(h, win_ref[:, o0:o1])
    k = zkv[:, :KV_WIDTH]
    v = zkv[:, KV_WIDTH:]
    half = ATTN_WIDTH // 2
    q_ms = _head_mean_square(zq)
    k_ms = _head_mean_square(k)
    zc = _dot(h, win_ref[:, o1:])
    gq = jnp.tile(gq_ref[...], (1, N_HEADS)) * (HEAD_DIM ** -0.5 * LOG2E)
    gk = jnp.tile(gk_ref[...], (1, N_KV_HEADS))
    qn = zq * lax.rsqrt(q_ms + EPS) * gq
    kn = k * lax.rsqrt(k_ms + EPS) * gk
    col = lax.broadcasted_iota(jnp.int32, (tm, ATTN_WIDTH), 1)
    in_a = ((col % LANES) < HEAD_DIM) == (col < half)
    qa = jnp.where(in_a, qn, 0.0).astype(jnp.bfloat16)
    qb = jnp.where(in_a, 0.0, qn).astype(jnp.bfloat16)
    bf = lambda t: t.astype(jnp.bfloat16)
    kv = jnp.concatenate([bf(kn), bf(pltpu.roll(kn, HEAD_DIM, 1)),
                          bf(v), bf(pltpu.roll(v, HEAD_DIM, 1))], axis=1)
    kv_prev = kv_tail[...]
    kv_tail[...] = kv[tm - WINDOW:]

    u = zc[:, CONV_WIDTH:2 * CONV_WIDTH] * zc[:, 2 * CONV_WIDTH:]
    ext = jnp.concatenate([jnp.where(first, 0.0, u_tail[...]), u], axis=0)
    u1 = pltpu.roll(ext, 1, 0)[SUBLANES:]
    u2 = pltpu.roll(ext, 2, 0)[SUBLANES:]
    u_tail[...] = u[tm - SUBLANES:]
    conv = cw_ref[0:1, :] * u2 + cw_ref[1:2, :] * u1 + cw_ref[2:3, :] * u
    yc = zc[:, :CONV_WIDTH] * conv
    ybuf[:, ATTN_WIDTH:] = (yc * _rms_scale(yc, CONV_WIDTH) * gc_ref[...]).astype(jnp.bfloat16)

    lane = lax.broadcasted_iota(jnp.int32, (WINDOW, LANES), 1)
    lo = lane < HEAD_DIM
    qi = lax.broadcasted_iota(jnp.int32, (WINDOW, 2 * WINDOW), 0)
    kj = lax.broadcasted_iota(jnp.int32, (WINDOW, 2 * WINDOW), 1)
    band = (kj > qi) & (kj <= qi + WINDOW)
    band_first = band & (kj >= jnp.where(first, WINDOW, 0))
    band4 = jnp.concatenate([band] * 4, axis=0)
    band4_first = jnp.concatenate([band_first] * 4, axis=0)

    def sink_rows(heads):
        return jnp.concatenate(
            [jnp.full((WINDOW, LANES), sinks_ref[hd] * LOG2E, jnp.float32) for hd in heads], axis=0)

    sink_a = sink_rows(GROUP_A_HEADS)
    sink_b = sink_rows(GROUP_B_HEADS)
    ones = jnp.ones((2 * WINDOW, LANES), jnp.bfloat16)

    def scores(qg, r0, kk, sink, mask):
        qs = jnp.concatenate(
            [qg[r0:r0 + WINDOW, sl * LANES:(sl + 1) * LANES] for sl in range(4)], axis=0)
        s = lax.dot_general(qs, kk, (((1,), (1,)), ((), ())),
                            preferred_element_type=jnp.float32)
        s = jnp.where(mask, s, NEG_INF)
        m = jnp.max(jnp.maximum(s[:, :LANES], s[:, LANES:]), axis=-1, keepdims=True)
        m = jnp.maximum(jnp.broadcast_to(m, sink.shape), sink)
        p = jnp.exp2(s - jnp.concatenate([m, m], axis=1))
        return p.astype(jnp.bfloat16), m

    def weighted(p, m, vv, sink):
        ov = _dot(p, jnp.concatenate([vv, ones], axis=1))
        l = ov[:, LANES:] + jnp.exp2(sink - m)
        return ov[:, :LANES] / l

    def kv_window(j):
        if j == 0:
            return jnp.concatenate([kv_prev, kv[0:WINDOW]], axis=0)
        return kv[(j - 1) * WINDOW:(j + 1) * WINDOW]

    def block_scores(j):
        kvw = kv_window(j)
        mask = band4_first if j == 0 else band4
        return (scores(qa, j * WINDOW, kvw[:, 0:LANES], sink_a, mask),
                scores(qb, j * WINDOW, kvw[:, LANES:2 * LANES], sink_b, mask))

    def block_output(j, pm):
        kvw = kv_window(j)
        (p_a, m_a), (p_b, m_b) = pm
        o_a = weighted(p_a, m_a, kvw[:, 2 * LANES:3 * LANES], sink_a)
        o_b = weighted(p_b, m_b, kvw[:, 3 * LANES:4 * LANES], sink_b)
        rows = lambda o, sl: o[sl * WINDOW:(sl + 1) * WINDOW]
        y = jnp.concatenate([jnp.where(lo, rows(o_a, 0), rows(o_b, 0)),
                             jnp.where(lo, rows(o_a, 1), rows(o_b, 1)),
                             jnp.where(lo, rows(o_b, 2), rows(o_a, 2)),
                             jnp.where(lo, rows(o_b, 3), rows(o_a, 3))], axis=1)
        y = y * _rms_scale(y, ATTN_WIDTH) * ga_ref[...]
        ybuf[j * WINDOW:(j + 1) * WINDOW, 0:ATTN_WIDTH] = y.astype(jnp.bfloat16)

    n_blocks = tm // WINDOW
    mlp_pieces(MLP_PIECES_BEFORE_ATTENTION)
    pending = block_scores(0)
    for j in range(n_blocks):
        upcoming = block_scores(j + 1) if j + 1 < n_blocks else None
        mlp_pieces(MLP_PIECES_PER_BLOCK)
        block_output(j, pending)
        pending = upcoming
    x1 = x + _dot(ybuf[:, ATTN_WIDTH:], wo_ref[ATTN_WIDTH:, :])
    mlp_pieces(MLP_PIECES_BEFORE_OUT_PROJ)
    x1 = x1 + _dot(ybuf[:, :ATTN_WIDTH], wo_ref[:ATTN_WIDTH, :])
    mlp_pieces(len(pieces))

    x1_prev[...] = x1
    h_prev[...] = (x1 * _rms_scale(x1, D_MODEL) * g2_ref[...]).astype(jnp.bfloat16)


def _layer(x2, sinks, g1, w_in, gq, gk, conv_w, gc, ga, w_out, g2, w_up, w_down, seq_len):
    t = x2.shape[0]
    tm = TOKEN_TILE
    n_tiles = t // tm
    const = lambda shape: pl.BlockSpec(
        shape, lambda i, *_: (0,) * len(shape), pipeline_mode=pl.Buffered(1))
    grid_spec = pltpu.PrefetchScalarGridSpec(
        num_scalar_prefetch=1,
        grid=(n_tiles + 1,),
        in_specs=[
            pl.BlockSpec((tm, D_MODEL), lambda i, *_: (jnp.minimum(i, n_tiles - 1), 0)),
            const((1, D_MODEL)), const((1, HEAD_DIM)), const((1, HEAD_DIM)),
            pl.BlockSpec((None, CONV_K, CONV_WIDTH), lambda i, *_: (0, 0, 0),
                         pipeline_mode=pl.Buffered(1)),
            const((1, CONV_WIDTH)), const((1, ATTN_WIDTH)), const((1, D_MODEL)),
        ] + [pl.BlockSpec(memory_space=pl.ANY)] * 4,
        out_specs=pl.BlockSpec((tm, D_MODEL), lambda i, *_: (jnp.maximum(i - 1, 0), 0)),
        scratch_shapes=[
            pltpu.VMEM((tm, D_MODEL), jnp.float32),
            pltpu.VMEM((tm, D_MODEL), jnp.bfloat16),
            pltpu.VMEM((WINDOW, 4 * LANES), jnp.bfloat16),
            pltpu.VMEM((SUBLANES, CONV_WIDTH), jnp.float32),
            pltpu.VMEM((tm, D_MODEL), jnp.bfloat16),
            pltpu.VMEM((D_MODEL, IN_COLS), jnp.bfloat16),
            pltpu.VMEM((D_MODEL, D_MODEL), jnp.bfloat16),
            pltpu.VMEM((D_MODEL, D_FF), jnp.bfloat16),
            pltpu.VMEM((D_FF, D_MODEL), jnp.bfloat16),
            pltpu.VMEM((WEIGHT_STAGE_SLOTS,) + WEIGHT_STAGE_BLOCK, jnp.float32),
            pltpu.SemaphoreType.DMA((WEIGHT_STAGE_SLOTS,)),
        ],
    )
    return pl.pallas_call(
        functools.partial(_layer_kernel, seq_len // tm, n_tiles),
        out_shape=jax.ShapeDtypeStruct((t, D_MODEL), jnp.float32),
        grid_spec=grid_spec,
        compiler_params=pltpu.CompilerParams(
            dimension_semantics=("arbitrary",), vmem_limit_bytes=VMEM_LIMIT_BYTES),
        name="layer",
    )(sinks, x2, g1, gq, gk, conv_w, gc, ga, g2, w_in, w_out, w_up, w_down)


def kernel(x, attn_norm_g, w_in, q_norm_g, k_norm_g, sinks, conv_w, attn_out_g, conv_out_g,
           w_out, mlp_norm_g, w_up, w_down):
    b, s, d = x.shape
    depth = w_in.shape[0]
    assert d == D_MODEL and s % TOKEN_TILE == 0
    x2 = x.reshape(b * s, d)
    for l in range(depth):
        row = lambda p: p[l:l + 1]
        x2 = _layer(x2, sinks[l], row(attn_norm_g), w_in[l], row(q_norm_g), row(k_norm_g),
                    row(conv_w), row(conv_out_g), row(attn_out_g), w_out[l], row(mlp_norm_g),
                    w_up[l], w_down[l], s)
    return x2.reshape(b, s, d)
```

```python
import functools
import math

import jax
import jax.numpy as jnp
from jax import lax
from jax.experimental import pallas as pl
from jax.experimental.pallas import tpu as pltpu

D_MODEL = 1024
N_HEADS = 8
HEAD_DIM = 64
N_KV_HEADS = 2
ATTN_WIDTH = N_HEADS * HEAD_DIM
KV_WIDTH = N_KV_HEADS * HEAD_DIM
CONV_WIDTH = D_MODEL - ATTN_WIDTH
CONV_K = 3
WINDOW = 128
IN_COLS = ATTN_WIDTH + 2 * KV_WIDTH + 3 * CONV_WIDTH
D_FF = 4 * D_MODEL
EPS = 1e-6
NEG_INF = -1e30
LOG2E = math.log2(math.e)

LANES = 128
SUBLANES = 8
TOKEN_TILE = 1024
FF_CHUNK = 1024
MXU_COLS = 256
MLP_PIECES_BEFORE_ATTENTION = 2
MLP_PIECES_PER_BLOCK = 1
MLP_PIECES_BEFORE_OUT_PROJ = 2
WEIGHT_STAGE_BLOCK = (256, 1024)
WEIGHT_STAGE_SLOTS = TOKEN_TILE // WEIGHT_STAGE_BLOCK[0]
WEIGHT_JOBS_PER_PIECE = 2
VMEM_LIMIT_BYTES = 62 * 1024 * 1024

GROUP_A_HEADS = (0, 2, 5, 7)
GROUP_B_HEADS = (1, 3, 4, 6)


def _rms_scale(xf, width):
    return lax.rsqrt(jnp.sum(xf * xf, axis=-1, keepdims=True) * (1.0 / width) + EPS)


def _dot(a, b):
    return jnp.dot(a, b, preferred_element_type=jnp.float32)


def _head_mean_square(z):
    lo = lax.broadcasted_iota(jnp.int32, (z.shape[0], LANES), 1) < HEAD_DIM
    out = []
    for s in range(z.shape[1] // LANES):
        sq = z[:, s * LANES:(s + 1) * LANES]
        sq = sq * sq
        lo_sum = jnp.sum(jnp.where(lo, sq, 0.0), axis=-1, keepdims=True)
        hi_sum = jnp.sum(jnp.where(lo, 0.0, sq), axis=-1, keepdims=True)
        out.append(jnp.where(lo, lo_sum, hi_sum) * (1.0 / HEAD_DIM))
    return jnp.concatenate(out, axis=1) if len(out) > 1 else out[0]


def _weight_jobs(hbm_refs, vmem_refs, stage, sem):
    rows, cols = WEIGHT_STAGE_BLOCK
    slots = stage.shape[0] // rows
    assert stage.shape[1] == cols and all(src.shape[0] % rows == 0 for src in hbm_refs)
    jobs = [(src, dst, r0, c0, min(cols, src.shape[1] - c0))
            for src, dst in zip(hbm_refs, vmem_refs)
            for r0 in range(0, src.shape[0], rows) for c0 in range(0, src.shape[1], cols)]

    def copy(j):
        src, _, r0, c0, nc = jobs[j]
        slot = pl.ds((j % slots) * rows, rows)
        return pltpu.make_async_copy(src.at[pl.ds(r0, rows), pl.ds(c0, nc)],
                                     stage.at[slot, pl.ds(0, nc)], sem.at[j % slots])

    def run(j):
        _, dst, r0, c0, nc = jobs[j]
        s0 = (j % slots) * rows
        copy(j).wait()
        dst[r0:r0 + rows, c0:c0 + nc] = stage[s0:s0 + rows, 0:nc].astype(jnp.bfloat16)
        if j + slots < len(jobs):
            copy(j + slots).start()

    for j in range(min(slots, len(jobs))):
        copy(j).start()
    return [functools.partial(run, j) for j in range(len(jobs))]


def _layer_kernel(tiles_per_seq, n_tiles, sinks_ref, x_ref, g1_ref, gq_ref, gk_ref, cw_ref, gc_ref,
                  ga_ref, g2_ref, win_hbm, wo_hbm, wu_hbm, wd_hbm, o_ref,
                  x1_prev, h_prev, kv_tail, u_tail, ybuf, win_ref, wo_ref, wu_ref, wd_ref, sem):
    step = pl.program_id(0)
    stage = x1_prev
    refs = (sinks_ref, x_ref, g1_ref, win_ref, gq_ref, gk_ref, cw_ref, gc_ref,
            ga_ref, wo_ref, g2_ref, wu_ref, wd_ref, o_ref, x1_prev, h_prev, kv_tail, u_tail, ybuf)

    @pl.when(step == 0)
    def _():
        kv_tail[...] = jnp.zeros_like(kv_tail)
        u_tail[...] = jnp.zeros_like(u_tail)
        for job in _weight_jobs((win_hbm, wo_hbm), (win_ref, wo_ref), stage, sem):
            job()
        _step(True, _weight_jobs((wu_hbm, wd_hbm), (wu_ref, wd_ref), stage, sem), True, *refs)

    @pl.when(jnp.logical_and(step > 0, step < n_tiles))
    def _():
        _step(True, None, (step % tiles_per_seq) == 0, *refs)

    @pl.when(step == n_tiles)
    def _():
        _step(False, None, False, *refs)


def _step(with_mixer, fillers, first, sinks_ref, x_ref, g1_ref, win_ref, gq_ref, gk_ref,
          cw_ref, gc_ref, ga_ref, wo_ref, g2_ref, wu_ref, wd_ref,
          o_ref, x1_prev, h_prev, kv_tail, u_tail, ybuf):
    tm = TOKEN_TILE
    with_mlp = fillers is None

    if with_mlp:
        o_ref[...] = x1_prev[...]
    n_up = FF_CHUNK // MXU_COLS
    n_down = D_MODEL // MXU_COLS
    acts = {}

    def up_piece(c, n):
        w0 = c * FF_CHUNK + n * MXU_COLS
        a = jnp.maximum(_dot(h_prev[...], wu_ref[:, w0:w0 + MXU_COLS]), 0.0)
        acts[c, n] = (a * a).astype(jnp.bfloat16)

    def down_piece(c, n):
        a = jnp.concatenate([acts[c, i] for i in range(n_up)], axis=1)
        o_ref[:, n * MXU_COLS:(n + 1) * MXU_COLS] += _dot(
            a, wd_ref[c * FF_CHUNK:(c + 1) * FF_CHUNK, n * MXU_COLS:(n + 1) * MXU_COLS])

    pieces = list(fillers) if fillers is not None else [
        functools.partial(f, c, n) for c in range(D_FF // FF_CHUNK)
        for f, count in ((up_piece, n_up), (down_piece, n_down)) for n in range(count)]

    per_site = 1 if fillers is None else WEIGHT_JOBS_PER_PIECE

    def mlp_pieces(count):
        for _ in range(min(count * per_site, len(pieces))):
            pieces.pop(0)()

    if not with_mixer:
        mlp_pieces(len(pieces))
        return
    mlp_pieces(4)

    x = x_ref[...]
    h = (x * _rms_scale(x, D_MODEL) * g1_ref[...]).astype(jnp.bfloat16)
    o0 = ATTN_WIDTH
    o1 = o0 + 2 * KV_WIDTH
    zq = _dot(h, win_ref[:, :o0])
    zkv = _dot(h, win_ref[:, o0:o1])
    k = zkv[:, :KV_WIDTH]
    v = zkv[:, KV_WIDTH:]
    half = ATTN_WIDTH // 2
    q_ms = _head_mean_square(zq)
    k_ms = _head_mean_square(k)
    zc = _dot(h, win_ref[:, o1:])
    gq = jnp.tile(gq_ref[...], (1, N_HEADS)) * (HEAD_DIM ** -0.5 * LOG2E)
    gk = jnp.tile(gk_ref[...], (1, N_KV_HEADS))
    qn = zq * lax.rsqrt(q_ms + EPS) * gq
    kn = k * lax.rsqrt(k_ms + EPS) * gk
    col = lax.broadcasted_iota(jnp.int32, (tm, ATTN_WIDTH), 1)
    in_a = ((col % LANES) < HEAD_DIM) == (col < half)
    qa = jnp.where(in_a, qn, 0.0).astype(jnp.bfloat16)
    qb = jnp.where(in_a, 0.0, qn).astype(jnp.bfloat16)
    bf = lambda t: t.astype(jnp.bfloat16)
    kv = jnp.concatenate([bf(kn), bf(pltpu.roll(kn, HEAD_DIM, 1)),
                          bf(v), bf(pltpu.roll(v, HEAD_DIM, 1))], axis=1)
    kv_prev = kv_tail[...]
    kv_tail[...] = kv[tm - WINDOW:]

    u = zc[:, CONV_WIDTH:2 * CONV_WIDTH] * zc[:, 2 * CONV_WIDTH:]
    ext = jnp.concatenate([jnp.where(first, 0.0, u_tail[...]), u], axis=0)
    u1 = pltpu.roll(ext, 1, 0)[SUBLANES:]
    u2 = pltpu.roll(ext, 2, 0)[SUBLANES:]
    u_tail[...] = u[tm - SUBLANES:]
    conv = cw_ref[0:1, :] * u2 + cw_ref[1:2, :] * u1 + cw_ref[2:3, :] * u
    yc = zc[:, :CONV_WIDTH] * conv
    ybuf[:, ATTN_WIDTH:] = (yc * _rms_scale(yc, CONV_WIDTH) * gc_ref[...]).astype(jnp.bfloat16)

    lane = lax.broadcasted_iota(jnp.int32, (WINDOW, LANES), 1)
    lo = lane < HEAD_DIM
    qi = lax.broadcasted_iota(jnp.int32, (WINDOW, 2 * WINDOW), 0)
    kj = lax.broadcasted_iota(jnp.int32, (WINDOW, 2 * WINDOW), 1)
    band = (kj > qi) & (kj <= qi + WINDOW)
    band_first = band & (kj >= jnp.where(first, WINDOW, 0))
    band4 = jnp.concatenate([band] * 4, axis=0)
    band4_first = jnp.concatenate([band_first] * 4, axis=0)

    def sink_rows(heads):
        return jnp.concatenate(
            [jnp.full((WINDOW, LANES), sinks_ref[hd] * LOG2E, jnp.float32) for hd in heads], axis=0)

    sink_a = sink_rows(GROUP_A_HEADS)
    sink_b = sink_rows(GROUP_B_HEADS)
    ones = jnp.ones((2 * WINDOW, LANES), jnp.bfloat16)

    def scores(qg, r0, kk, sink, mask):
        qs = jnp.concatenate(
            [qg[r0:r0 + WINDOW, sl * LANES:(sl + 1) * LANES] for sl in range(4)], axis=0)
        s = lax.dot_general(qs, kk, (((1,), (1,)), ((), ())),
                            preferred_element_type=jnp.float32)
        s = jnp.where(mask, s, NEG_INF)
        m = jnp.max(jnp.maximum(s[:, :LANES], s[:, LANES:]), axis=-1, keepdims=True)
        m = jnp.maximum(jnp.broadcast_to(m, sink.shape), sink)
        p = jnp.exp2(s - jnp.concatenate([m, m], axis=1))
        return p.astype(jnp.bfloat16), m

    def weighted(p, m, vv, sink):
        ov = _dot(p, jnp.concatenate([vv, ones], axis=1))
        l = ov[:, LANES:] + jnp.exp2(sink - m)
        return ov[:, :LANES] / l

    def kv_window(j):
        if j == 0:
            return jnp.concatenate([kv_prev, kv[0:WINDOW]], axis=0)
        return kv[(j - 1) * WINDOW:(j + 1) * WINDOW]

    def block_scores(j):
        kvw = kv_window(j)
        mask = band4_first if j == 0 else band4
        return (scores(qa, j * WINDOW, kvw[:, 0:LANES], sink_a, mask),
                scores(qb, j * WINDOW, kvw[:, LANES:2 * LANES], sink_b, mask))

    def block_output(j, pm):
        kvw = kv_window(j)
        (p_a, m_a), (p_b, m_b) = pm
        o_a = weighted(p_a, m_a, kvw[:, 2 * LANES:3 * LANES], sink_a)
        o_b = weighted(p_b, m_b, kvw[:, 3 * LANES:4 * LANES], sink_b)
        rows = lambda o, sl: o[sl * WINDOW:(sl + 1) * WINDOW]
        y = jnp.concatenate([jnp.where(lo, rows(o_a, 0), rows(o_b, 0)),
                             jnp.where(lo, rows(o_a, 1), rows(o_b, 1)),
                             jnp.where(lo, rows(o_b, 2), rows(o_a, 2)),
                             jnp.where(lo, rows(o_b, 3), rows(o_a, 3))], axis=1)
        y = y * _rms_scale(y, ATTN_WIDTH) * ga_ref[...]
        ybuf[j * WINDOW:(j + 1) * WINDOW, 0:ATTN_WIDTH] = y.astype(jnp.bfloat16)

    n_blocks = tm // WINDOW
    mlp_pieces(MLP_PIECES_BEFORE_ATTENTION)
    pending = block_scores(0)
    for j in range(n_blocks):
        upcoming = block_scores(j + 1) if j + 1 < n_blocks else None
        mlp_pieces(MLP_PIECES_PER_BLOCK)
        block_output(j, pending)
        pending = upcoming
    x1 = x + _dot(ybuf[:, ATTN_WIDTH:], wo_ref[ATTN_WIDTH:, :])
    mlp_pieces(MLP_PIECES_BEFORE_OUT_PROJ)
    x1 = x1 + _dot(ybuf[:, :ATTN_WIDTH], wo_ref[:ATTN_WIDTH, :])
    mlp_pieces(len(pieces))

    x1_prev[...] = x1
    h_prev[...] = (x1 * _rms_scale(x1, D_MODEL) * g2_ref[...]).astype(jnp.bfloat16)


def _layer(x2, sinks, g1, w_in, gq, gk, conv_w, gc, ga, w_out, g2, w_up, w_down, seq_len):
    t = x2.shape[0]
    tm = TOKEN_TILE
    n_tiles = t // tm
    const = lambda shape: pl.BlockSpec(
        shape, lambda i, *_: (0,) * len(shape), pipeline_mode=pl.Buffered(1))
    grid_spec = pltpu.PrefetchScalarGridSpec(
        num_scalar_prefetch=1,
        grid=(n_tiles + 1,),
        in_specs=[
            pl.BlockSpec((tm, D_MODEL), lambda i, *_: (jnp.minimum(i, n_tiles - 1), 0)),
            const((1, D_MODEL)), const((1, HEAD_DIM)), const((1, HEAD_DIM)),
            pl.BlockSpec((None, CONV_K, CONV_WIDTH), lambda i, *_: (0, 0, 0),
                         pipeline_mode=pl.Buffered(1)),
            const((1, CONV_WIDTH)), const((1, ATTN_WIDTH)), const((1, D_MODEL)),
        ] + [pl.BlockSpec(memory_space=pl.ANY)] * 4,
        out_specs=pl.BlockSpec((tm, D_MODEL), lambda i, *_: (jnp.maximum(i - 1, 0), 0)),
        scratch_shapes=[
            pltpu.VMEM((tm, D_MODEL), jnp.float32),
            pltpu.VMEM((tm, D_MODEL), jnp.bfloat16),
            pltpu.VMEM((WINDOW, 4 * LANES), jnp.bfloat16),
            pltpu.VMEM((SUBLANES, CONV_WIDTH), jnp.float32),
            pltpu.VMEM((tm, D_MODEL), jnp.bfloat16),
            pltpu.VMEM((D_MODEL, IN_COLS), jnp.bfloat16),
            pltpu.VMEM((D_MODEL, D_MODEL), jnp.bfloat16),
            pltpu.VMEM((D_MODEL, D_FF), jnp.bfloat16),
            pltpu.VMEM((D_FF, D_MODEL), jnp.bfloat16),
            pltpu.SemaphoreType.DMA((WEIGHT_STAGE_SLOTS,)),
        ],
    )
    return pl.pallas_call(
        functools.partial(_layer_kernel, seq_len // tm, n_tiles),
        out_shape=jax.ShapeDtypeStruct((t, D_MODEL), jnp.float32),
        grid_spec=grid_spec,
        compiler_params=pltpu.CompilerParams(
            dimension_semantics=("arbitrary",), vmem_limit_bytes=VMEM_LIMIT_BYTES),
        name="layer",
    )(sinks, x2, g1, gq, gk, conv_w, gc, ga, g2, w_in, w_out, w_up, w_down)


def kernel(x, attn_norm_g, w_in, q_norm_g, k_norm_g, sinks, conv_w, attn_out_g, conv_out_g,
           w_out, mlp_norm_g, w_up, w_down):
    b, s, d = x.shape
    depth = w_in.shape[0]
    assert d == D_MODEL and s % TOKEN_TILE == 0
    x2 = x.reshape(b * s, d)
    for l in range(depth):
        row = lambda p: p[l:l + 1]
        x2 = _layer(x2, sinks[l], row(attn_norm_g), w_in[l], row(q_norm_g), row(k_norm_g),
                    row(conv_w), row(conv_out_g), row(attn_out_g), w_out[l], row(mlp_norm_g),
                    w_up[l], w_down[l], s)
    return x2.reshape(b, s, d)
```

```python
import functools
import math

import jax
import jax.numpy as jnp
from jax import lax
from jax.experimental import pallas as pl
from jax.experimental.pallas import tpu as pltpu

D_MODEL = 1024
N_HEADS = 8
HEAD_DIM = 64
N_KV_HEADS = 2
ATTN_WIDTH = N_HEADS * HEAD_DIM
KV_WIDTH = N_KV_HEADS * HEAD_DIM
CONV_WIDTH = D_MODEL - ATTN_WIDTH
CONV_K = 3
WINDOW = 128
IN_COLS = ATTN_WIDTH + 2 * KV_WIDTH + 3 * CONV_WIDTH
D_FF = 4 * D_MODEL
EPS = 1e-6
NEG_INF = -1e30
LOG2E = math.log2(math.e)

LANES = 128
SUBLANES = 8
TOKEN_TILE = 512
FF_CHUNK = 1024
MXU_COLS = 256
MLP_PIECES_BEFORE_ATTENTION = 2
MLP_PIECES_PER_BLOCK = 2
MLP_PIECES_BEFORE_OUT_PROJ = 2
WEIGHT_STAGE_BLOCK = (512, 1024)
WEIGHT_STAGE_SLOTS = 4
VMEM_LIMIT_BYTES = 56 * 1024 * 1024

GROUP_A_HEADS = (0, 2, 5, 7)
GROUP_B_HEADS = (1, 3, 4, 6)


def _rms_scale(xf, width):
    return lax.rsqrt(jnp.sum(xf * xf, axis=-1, keepdims=True) * (1.0 / width) + EPS)


def _dot(a, b):
    return jnp.dot(a, b, preferred_element_type=jnp.float32)


def _head_mean_square(z):
    lo = lax.broadcasted_iota(jnp.int32, (z.shape[0], LANES), 1) < HEAD_DIM
    out = []
    for s in range(z.shape[1] // LANES):
        sq = z[:, s * LANES:(s + 1) * LANES]
        sq = sq * sq
        lo_sum = jnp.sum(jnp.where(lo, sq, 0.0), axis=-1, keepdims=True)
        hi_sum = jnp.sum(jnp.where(lo, 0.0, sq), axis=-1, keepdims=True)
        out.append(jnp.where(lo, lo_sum, hi_sum) * (1.0 / HEAD_DIM))
    return jnp.concatenate(out, axis=1) if len(out) > 1 else out[0]


def _weight_jobs(hbm_refs, vmem_refs, stage, sem):
    slots, rows, cols = stage.shape
    jobs = [(src, dst, r0, c0, min(cols, src.shape[1] - c0))
            for src, dst in zip(hbm_refs, vmem_refs)
            for r0 in range(0, src.shape[0], rows) for c0 in range(0, src.shape[1], cols)]

    def copy(j):
        src, _, r0, c0, nc = jobs[j]
        return pltpu.make_async_copy(src.at[pl.ds(r0, rows), pl.ds(c0, nc)],
                                     stage.at[j % slots, :, pl.ds(0, nc)], sem.at[j % slots])

    def run(j):
        _, dst, r0, c0, nc = jobs[j]
        copy(j).wait()
        dst[r0:r0 + rows, c0:c0 + nc] = stage[j % slots, :, 0:nc].astype(jnp.bfloat16)
        if j + slots < len(jobs):
            copy(j + slots).start()

    for j in range(min(slots, len(jobs))):
        copy(j).start()
    return [functools.partial(run, j) for j in range(len(jobs))]


def _layer_kernel(tiles_per_seq, n_tiles, sinks_ref, x_ref, g1_ref, gq_ref, gk_ref, cw_ref, gc_ref,
                  ga_ref, g2_ref, win_hbm, wo_hbm, wu_hbm, wd_hbm, o_ref,
                  x1_prev, h_prev, kv_tail, u_tail, ybuf, win_ref, wo_ref, wu_ref, wd_ref,
                  stage, sem):
    step = pl.program_id(0)
    refs = (sinks_ref, x_ref, g1_ref, win_ref, gq_ref, gk_ref, cw_ref, gc_ref,
            ga_ref, wo_ref, g2_ref, wu_ref, wd_ref, o_ref, x1_prev, h_prev, kv_tail, u_tail, ybuf)

    @pl.when(step == 0)
    def _():
        kv_tail[...] = jnp.zeros_like(kv_tail)
        u_tail[...] = jnp.zeros_like(u_tail)
        for job in _weight_jobs((win_hbm, wo_hbm), (win_ref, wo_ref), stage, sem):
            job()
        _step(True, _weight_jobs((wu_hbm, wd_hbm), (wu_ref, wd_ref), stage, sem), True, *refs)

    @pl.when(jnp.logical_and(step > 0, step < n_tiles))
    def _():
        _step(True, None, (step % tiles_per_seq) == 0, *refs)

    @pl.when(step == n_tiles)
    def _():
        _step(False, None, False, *refs)


def _step(with_mixer, fillers, first, sinks_ref, x_ref, g1_ref, win_ref, gq_ref, gk_ref,
          cw_ref, gc_ref, ga_ref, wo_ref, g2_ref, wu_ref, wd_ref,
          o_ref, x1_prev, h_prev, kv_tail, u_tail, ybuf):
    tm = TOKEN_TILE
    with_mlp = fillers is None

    if with_mlp:
        o_ref[...] = x1_prev[...]
    n_up = FF_CHUNK // MXU_COLS
    n_down = D_MODEL // MXU_COLS
    acts = {}

    def up_piece(c, n):
        w0 = c * FF_CHUNK + n * MXU_COLS
        a = jnp.maximum(_dot(h_prev[...], wu_ref[:, w0:w0 + MXU_COLS]), 0.0)
        acts[c, n] = (a * a).astype(jnp.bfloat16)

    def down_piece(c, n):
        a = jnp.concatenate([acts[c, i] for i in range(n_up)], axis=1)
        o_ref[:, n * MXU_COLS:(n + 1) * MXU_COLS] += _dot(
            a, wd_ref[c * FF_CHUNK:(c + 1) * FF_CHUNK, n * MXU_COLS:(n + 1) * MXU_COLS])

    pieces = list(fillers) if fillers is not None else [
        functools.partial(f, c, n) for c in range(D_FF // FF_CHUNK)
        for f, count in ((up_piece, n_up), (down_piece, n_down)) for n in range(count)]

    def mlp_pieces(count):
        for _ in range(min(count, len(pieces))):
            pieces.pop(0)()

    if not with_mixer:
        mlp_pieces(len(pieces))
        return
    mlp_pieces(4)

    x = x_ref[...]
    h = (x * _rms_scale(x, D_MODEL) * g1_ref[...]).astype(jnp.bfloat16)
    o0 = ATTN_WIDTH
    o1 = o0 + 2 * KV_WIDTH
    zq = _dot(h, win_ref[:, :o0])
    zkv = _dot(h, win_ref[:, o0:o1])
    k = zkv[:, :KV_WIDTH]
    v = zkv[:, KV_WIDTH:]
    half = ATTN_WIDTH // 2
    q_ms = _head_mean_square(zq)
    k_ms = _head_mean_square(k)
    zc = _dot(h, win_ref[:, o1:])
    gq = jnp.tile(gq_ref[...], (1, N_HEADS)) * (HEAD_DIM ** -0.5 * LOG2E)
    gk = jnp.tile(gk_ref[...], (1, N_KV_HEADS))
    qn = zq * lax.rsqrt(q_ms + EPS) * gq
    kn = k * lax.rsqrt(k_ms + EPS) * gk
    col = lax.broadcasted_iota(jnp.int32, (tm, ATTN_WIDTH), 1)
    in_a = ((col % LANES) < HEAD_DIM) == (col < half)
    qa = jnp.where(in_a, qn, 0.0).astype(jnp.bfloat16)
    qb = jnp.where(in_a, 0.0, qn).astype(jnp.bfloat16)
    bf = lambda t: t.astype(jnp.bfloat16)
    kv = jnp.concatenate([bf(kn), bf(pltpu.roll(kn, HEAD_DIM, 1)),
                          bf(v), bf(pltpu.roll(v, HEAD_DIM, 1))], axis=1)
    kv_prev = kv_tail[...]
    kv_tail[...] = kv[tm - WINDOW:]

    u = zc[:, CONV_WIDTH:2 * CONV_WIDTH] * zc[:, 2 * CONV_WIDTH:]
    ext = jnp.concatenate([jnp.where(first, 0.0, u_tail[...]), u], axis=0)
    u1 = pltpu.roll(ext, 1, 0)[SUBLANES:]
    u2 = pltpu.roll(ext, 2, 0)[SUBLANES:]
    u_tail[...] = u[tm - SUBLANES:]
    conv = cw_ref[0:1, :] * u2 + cw_ref[1:2, :] * u1 + cw_ref[2:3, :] * u
    yc = zc[:, :CONV_WIDTH] * conv
    ybuf[:, ATTN_WIDTH:] = (yc * _rms_scale(yc, CONV_WIDTH) * gc_ref[...]).astype(jnp.bfloat16)

    lane = lax.broadcasted_iota(jnp.int32, (WINDOW, LANES), 1)
    lo = lane < HEAD_DIM
    qi = lax.broadcasted_iota(jnp.int32, (WINDOW, WINDOW), 0)
    kj = lax.broadcasted_iota(jnp.int32, (WINDOW, WINDOW), 1)
    from_prev = jnp.concatenate([kj > qi] * 4, axis=0)

    def sink_rows(heads):
        return jnp.concatenate(
            [jnp.full((WINDOW, LANES), sinks_ref[hd] * LOG2E, jnp.float32) for hd in heads], axis=0)

    sink_a = sink_rows(GROUP_A_HEADS)
    sink_b = sink_rows(GROUP_B_HEADS)
    ones = jnp.ones((2 * WINDOW, LANES), jnp.bfloat16)

    def scores(qg, r0, kk, sink, no_prev):
        qs = jnp.concatenate(
            [qg[r0:r0 + WINDOW, sl * LANES:(sl + 1) * LANES] for sl in range(4)], axis=0)
        s = lax.dot_general(qs, kk, (((1,), (1,)), ((), ())),
                            preferred_element_type=jnp.float32)
        s_prev = s[:, :LANES] if no_prev is None else jnp.where(no_prev, NEG_INF, s[:, :LANES])
        s = jnp.where(from_prev, s_prev, s[:, LANES:])
        m = jnp.max(s, axis=-1, keepdims=True)
        m = jnp.maximum(jnp.broadcast_to(m, sink.shape), sink)
        return jnp.exp2(s - m), m

    def weighted(p, m, vv, sink):
        p = jnp.concatenate([jnp.where(from_prev, p, 0.0), jnp.where(from_prev, 0.0, p)],
                            axis=1).astype(jnp.bfloat16)
        ov = _dot(p, jnp.concatenate([vv, ones], axis=1))
        l = ov[:, LANES:] + jnp.exp2(sink - m)
        return ov[:, :LANES] / l

    def kv_window(j):
        if j == 0:
            return jnp.concatenate([kv_prev, kv[0:WINDOW]], axis=0)
        return kv[(j - 1) * WINDOW:(j + 1) * WINDOW]

    def block_scores(j):
        kvw = kv_window(j)
        no_prev = first if j == 0 else None
        return (scores(qa, j * WINDOW, kvw[:, 0:LANES], sink_a, no_prev),
                scores(qb, j * WINDOW, kvw[:, LANES:2 * LANES], sink_b, no_prev))

    def block_output(j, pm):
        kvw = kv_window(j)
        (p_a, m_a), (p_b, m_b) = pm
        o_a = weighted(p_a, m_a, kvw[:, 2 * LANES:3 * LANES], sink_a)
        o_b = weighted(p_b, m_b, kvw[:, 3 * LANES:4 * LANES], sink_b)
        rows = lambda o, sl: o[sl * WINDOW:(sl + 1) * WINDOW]
        y = jnp.concatenate([jnp.where(lo, rows(o_a, 0), rows(o_b, 0)),
                             jnp.where(lo, rows(o_a, 1), rows(o_b, 1)),
                             jnp.where(lo, rows(o_b, 2), rows(o_a, 2)),
                             jnp.where(lo, rows(o_b, 3), rows(o_a, 3))], axis=1)
        y = y * _rms_scale(y, ATTN_WIDTH) * ga_ref[...]
        ybuf[j * WINDOW:(j + 1) * WINDOW, 0:ATTN_WIDTH] = y.astype(jnp.bfloat16)

    n_blocks = tm // WINDOW
    mlp_pieces(MLP_PIECES_BEFORE_ATTENTION)
    pending = block_scores(0)
    for j in range(n_blocks):
        upcoming = block_scores(j + 1) if j + 1 < n_blocks else None
        mlp_pieces(MLP_PIECES_PER_BLOCK)
        block_output(j, pending)
        pending = upcoming
    x1 = x + _dot(ybuf[:, ATTN_WIDTH:], wo_ref[ATTN_WIDTH:, :])
    mlp_pieces(MLP_PIECES_BEFORE_OUT_PROJ)
    x1 = x1 + _dot(ybuf[:, :ATTN_WIDTH], wo_ref[:ATTN_WIDTH, :])
    mlp_pieces(len(pieces))

    x1_prev[...] = x1
    h_prev[...] = (x1 * _rms_scale(x1, D_MODEL) * g2_ref[...]).astype(jnp.bfloat16)


def _layer(x2, sinks, g1, w_in, gq, gk, conv_w, gc, ga, w_out, g2, w_up, w_down, seq_len):
    t = x2.shape[0]
    tm = TOKEN_TILE
    n_tiles = t // tm
    const = lambda shape: pl.BlockSpec(
        shape, lambda i, *_: (0,) * len(shape), pipeline_mode=pl.Buffered(1))
    grid_spec = pltpu.PrefetchScalarGridSpec(
        num_scalar_prefetch=1,
        grid=(n_tiles + 1,),
        in_specs=[
            pl.BlockSpec((tm, D_MODEL), lambda i, *_: (jnp.minimum(i, n_tiles - 1), 0)),
            const((1, D_MODEL)), const((1, HEAD_DIM)), const((1, HEAD_DIM)),
            pl.BlockSpec((None, CONV_K, CONV_WIDTH), lambda i, *_: (0, 0, 0),
                         pipeline_mode=pl.Buffered(1)),
            const((1, CONV_WIDTH)), const((1, ATTN_WIDTH)), const((1, D_MODEL)),
        ] + [pl.BlockSpec(memory_space=pl.ANY)] * 4,
        out_specs=pl.BlockSpec((tm, D_MODEL), lambda i, *_: (jnp.maximum(i - 1, 0), 0)),
        scratch_shapes=[
            pltpu.VMEM((tm, D_MODEL), jnp.float32),
            pltpu.VMEM((tm, D_MODEL), jnp.bfloat16),
            pltpu.VMEM((WINDOW, 4 * LANES), jnp.bfloat16),
            pltpu.VMEM((SUBLANES, CONV_WIDTH), jnp.float32),
            pltpu.VMEM((tm, D_MODEL), jnp.bfloat16),
            pltpu.VMEM((D_MODEL, IN_COLS), jnp.bfloat16),
            pltpu.VMEM((D_MODEL, D_MODEL), jnp.bfloat16),
            pltpu.VMEM((D_MODEL, D_FF), jnp.bfloat16),
            pltpu.VMEM((D_FF, D_MODEL), jnp.bfloat16),
            pltpu.VMEM((WEIGHT_STAGE_SLOTS,) + WEIGHT_STAGE_BLOCK, jnp.float32),
            pltpu.SemaphoreType.DMA((WEIGHT_STAGE_SLOTS,)),
        ],
    )
    return pl.pallas_call(
        functools.partial(_layer_kernel, seq_len // tm, n_tiles),
        out_shape=jax.ShapeDtypeStruct((t, D_MODEL), jnp.float32),
        grid_spec=grid_spec,
        compiler_params=pltpu.CompilerParams(
            dimension_semantics=("arbitrary",), vmem_limit_bytes=VMEM_LIMIT_BYTES),
        name="layer",
    )(sinks, x2, g1, gq, gk, conv_w, gc, ga, g2, w_in, w_out, w_up, w_down)


def kernel(x, attn_norm_g, w_in, q_norm_g, k_norm_g, sinks, conv_w, attn_out_g, conv_out_g,
           w_out, mlp_norm_g, w_up, w_down):
    b, s, d = x.shape
    depth = w_in.shape[0]
    assert d == D_MODEL and s % TOKEN_TILE == 0
    x2 = x.reshape(b * s, d)
    for l in range(depth):
        row = lambda p: p[l:l + 1]
        x2 = _layer(x2, sinks[l], row(attn_norm_g), w_in[l], row(q_norm_g), row(k_norm_g),
                    row(conv_w), row(conv_out_g), row(attn_out_g), w_out[l], row(mlp_norm_g),
                    w_up[l], w_down[l], s)
    return x2.reshape(b, s, d)
```

```python
import functools
import math

import jax
import jax.numpy as jnp
from jax import lax
from jax.experimental import pallas as pl
from jax.experimental.pallas import tpu as pltpu

D_MODEL = 1024
N_HEADS = 8
HEAD_DIM = 64
N_KV_HEADS = 2
ATTN_WIDTH = N_HEADS * HEAD_DIM
KV_WIDTH = N_KV_HEADS * HEAD_DIM
CONV_WIDTH = D_MODEL - ATTN_WIDTH
CONV_K = 3
WINDOW = 128
IN_COLS = ATTN_WIDTH + 2 * KV_WIDTH + 3 * CONV_WIDTH
D_FF = 4 * D_MODEL
EPS = 1e-6
NEG_INF = -1e30
LOG2E = math.log2(math.e)

LANES = 128
SUBLANES = 8
TOKEN_TILE = 512
FF_CHUNK = 1024
MXU_COLS = 256
MLP_PIECES_BEFORE_ATTENTION = 2
MLP_PIECES_PER_BLOCK = 2
MLP_PIECES_BEFORE_OUT_PROJ = 2
WEIGHT_STAGE_BLOCK = (512, 1024)
WEIGHT_STAGE_SLOTS = 4
VMEM_LIMIT_BYTES = 56 * 1024 * 1024

GROUP_A_HEADS = (0, 2, 5, 7)
GROUP_B_HEADS = (1, 3, 4, 6)


def _rms_scale(xf, width):
    return lax.rsqrt(jnp.sum(xf * xf, axis=-1, keepdims=True) * (1.0 / width) + EPS)


def _dot(a, b):
    return jnp.dot(a, b, preferred_element_type=jnp.float32)


def _head_mean_square(z):
    lo = lax.broadcasted_iota(jnp.int32, (z.shape[0], LANES), 1) < HEAD_DIM
    out = []
    for s in range(z.shape[1] // LANES):
        sq = z[:, s * LANES:(s + 1) * LANES]
        sq = sq * sq
        lo_sum = jnp.sum(jnp.where(lo, sq, 0.0), axis=-1, keepdims=True)
        hi_sum = jnp.sum(jnp.where(lo, 0.0, sq), axis=-1, keepdims=True)
        out.append(jnp.where(lo, lo_sum, hi_sum) * (1.0 / HEAD_DIM))
    return jnp.concatenate(out, axis=1) if len(out) > 1 else out[0]


def _weight_jobs(hbm_refs, vmem_refs, stage, sem):
    slots, rows, cols = stage.shape
    jobs = [(src, dst, r0, c0, min(cols, src.shape[1] - c0))
            for src, dst in zip(hbm_refs, vmem_refs)
            for r0 in range(0, src.shape[0], rows) for c0 in range(0, src.shape[1], cols)]

    def copy(j):
        src, _, r0, c0, nc = jobs[j]
        return pltpu.make_async_copy(src.at[pl.ds(r0, rows), pl.ds(c0, nc)],
                                     stage.at[j % slots, :, pl.ds(0, nc)], sem.at[j % slots])

    def run(j):
        _, dst, r0, c0, nc = jobs[j]
        copy(j).wait()
        dst[r0:r0 + rows, c0:c0 + nc] = stage[j % slots, :, 0:nc].astype(jnp.bfloat16)
        if j + slots < len(jobs):
            copy(j + slots).start()

    for j in range(min(slots, len(jobs))):
        copy(j).start()
    return [functools.partial(run, j) for j in range(len(jobs))]


def _layer_kernel(tiles_per_seq, n_tiles, sinks_ref, x_ref, g1_ref, gq_ref, gk_ref, cw_ref, gc_ref,
                  ga_ref, g2_ref, win_hbm, wo_hbm, wu_hbm, wd_hbm, o_ref,
                  x1_prev, h_prev, kv_tail, u_tail, ybuf, win_ref, wo_ref, wu_ref, wd_ref,
                  stage, sem):
    step = pl.program_id(0)
    refs = (sinks_ref, x_ref, g1_ref, win_ref, gq_ref, gk_ref, cw_ref, gc_ref,
            ga_ref, wo_ref, g2_ref, wu_ref, wd_ref, o_ref, x1_prev, h_prev, kv_tail, u_tail, ybuf)

    @pl.when(step == 0)
    def _():
        kv_tail[...] = jnp.zeros_like(kv_tail)
        u_tail[...] = jnp.zeros_like(u_tail)
        for job in _weight_jobs((win_hbm, wo_hbm), (win_ref, wo_ref), stage, sem):
            job()
        _step(True, _weight_jobs((wu_hbm, wd_hbm), (wu_ref, wd_ref), stage, sem), True, *refs)

    @pl.when(jnp.logical_and(step > 0, step < n_tiles))
    def _():
        _step(True, None, (step % tiles_per_seq) == 0, *refs)

    @pl.when(step == n_tiles)
    def _():
        _step(False, None, False, *refs)


def _step(with_mixer, fillers, first, sinks_ref, x_ref, g1_ref, win_ref, gq_ref, gk_ref,
          cw_ref, gc_ref, ga_ref, wo_ref, g2_ref, wu_ref, wd_ref,
          o_ref, x1_prev, h_prev, kv_tail, u_tail, ybuf):
    tm = TOKEN_TILE
    with_mlp = fillers is None

    if with_mlp:
        o_ref[...] = x1_prev[...]
    n_up = FF_CHUNK // MXU_COLS
    n_down = D_MODEL // MXU_COLS
    acts = {}

    def up_piece(c, n):
        w0 = c * FF_CHUNK + n * MXU_COLS
        a = jnp.maximum(_dot(h_prev[...], wu_ref[:, w0:w0 + MXU_COLS]), 0.0)
        acts[c, n] = (a * a).astype(jnp.bfloat16)

    def down_piece(c, n):
        a = jnp.concatenate([acts[c, i] for i in range(n_up)], axis=1)
        o_ref[:, n * MXU_COLS:(n + 1) * MXU_COLS] += _dot(
            a, wd_ref[c * FF_CHUNK:(c + 1) * FF_CHUNK, n * MXU_COLS:(n + 1) * MXU_COLS])

    pieces = list(fillers) if fillers is not None else [
        functools.partial(f, c, n) for c in range(D_FF // FF_CHUNK)
        for f, count in ((up_piece, n_up), (down_piece, n_down)) for n in range(count)]

    def mlp_pieces(count):
        for _ in range(min(count, len(pieces))):
            pieces.pop(0)()

    if not with_mixer:
        mlp_pieces(len(pieces))
        return
    mlp_pieces(4)

    x = x_ref[...]
    h = (x * _rms_scale(x, D_MODEL) * g1_ref[...]).astype(jnp.bfloat16)
    o0 = ATTN_WIDTH
    o1 = o0 + 2 * KV_WIDTH
    zq = _dot(h, win_ref[:, :o0])
    zkv = _dot(h, win_ref[:, o0:o1])
    k = zkv[:, :KV_WIDTH]
    v = zkv[:, KV_WIDTH:]
    half = ATTN_WIDTH // 2
    q_ms = _head_mean_square(zq)
    k_ms = _head_mean_square(k)
    zc = _dot(h, win_ref[:, o1:])
    gq = jnp.tile(gq_ref[...], (1, N_HEADS)) * (HEAD_DIM ** -0.5 * LOG2E)
    gk = jnp.tile(gk_ref[...], (1, N_KV_HEADS))
    qn = zq * lax.rsqrt(q_ms + EPS) * gq
    kn = k * lax.rsqrt(k_ms + EPS) * gk
    col = lax.broadcasted_iota(jnp.int32, (tm, ATTN_WIDTH), 1)
    in_a = ((col % LANES) < HEAD_DIM) == (col < half)
    qa = jnp.where(in_a, qn, 0.0).astype(jnp.bfloat16)
    qb = jnp.where(in_a, 0.0, qn).astype(jnp.bfloat16)
    bf = lambda t: t.astype(jnp.bfloat16)
    kv = jnp.concatenate([bf(kn), bf(pltpu.roll(kn, HEAD_DIM, 1)),
                          bf(v), bf(pltpu.roll(v, HEAD_DIM, 1))], axis=1)
    kv_prev = kv_tail[...]
    kv_tail[...] = kv[tm - WINDOW:]

    u = zc[:, CONV_WIDTH:2 * CONV_WIDTH] * zc[:, 2 * CONV_WIDTH:]
    ext = jnp.concatenate([jnp.where(first, 0.0, u_tail[...]), u], axis=0)
    u1 = pltpu.roll(ext, 1, 0)[SUBLANES:]
    u2 = pltpu.roll(ext, 2, 0)[SUBLANES:]
    u_tail[...] = u[tm - SUBLANES:]
    conv = cw_ref[0:1, :] * u2 + cw_ref[1:2, :] * u1 + cw_ref[2:3, :] * u
    yc = zc[:, :CONV_WIDTH] * conv
    ybuf[:, ATTN_WIDTH:] = (yc * _rms_scale(yc, CONV_WIDTH) * gc_ref[...]).astype(jnp.bfloat16)

    lane = lax.broadcasted_iota(jnp.int32, (WINDOW, LANES), 1)
    lo = lane < HEAD_DIM
    qi = lax.broadcasted_iota(jnp.int32, (WINDOW, WINDOW), 0)
    kj = lax.broadcasted_iota(jnp.int32, (WINDOW, WINDOW), 1)
    from_prev = jnp.concatenate([kj > qi] * 4, axis=0)

    def sink_rows(heads):
        return jnp.concatenate(
            [jnp.full((WINDOW, LANES), sinks_ref[hd] * LOG2E, jnp.float32) for hd in heads], axis=0)

    sink_a = sink_rows(GROUP_A_HEADS)
    sink_b = sink_rows(GROUP_B_HEADS)
    ones = jnp.ones((2 * WINDOW, LANES), jnp.bfloat16)

    def scores(qg, r0, kk, sink, no_prev):
        qs = jnp.concatenate(
            [qg[r0:r0 + WINDOW, sl * LANES:(sl + 1) * LANES] for sl in range(4)], axis=0)
        s = lax.dot_general(qs, kk, (((1,), (1,)), ((), ())),
                            preferred_element_type=jnp.float32)
        s_prev = s[:, :LANES] if no_prev is None else jnp.where(no_prev, NEG_INF, s[:, :LANES])
        s = jnp.where(from_prev, s_prev, s[:, LANES:])
        m = jnp.max(s, axis=-1, keepdims=True)
        m = jnp.maximum(jnp.broadcast_to(m, sink.shape), sink)
        return jnp.exp2(s - m).astype(jnp.bfloat16), m

    def weighted(p, m, vv, sink):
        zero = jnp.zeros_like(p)
        p = jnp.concatenate([jnp.where(from_prev, p, zero), jnp.where(from_prev, zero, p)],
                            axis=1)
        ov = _dot(p, jnp.concatenate([vv, ones], axis=1))
        l = ov[:, LANES:] + jnp.exp2(sink - m)
        return ov[:, :LANES] / l

    def kv_window(j):
        if j == 0:
            return jnp.concatenate([kv_prev, kv[0:WINDOW]], axis=0)
        return kv[(j - 1) * WINDOW:(j + 1) * WINDOW]

    def block_scores(j):
        kvw = kv_window(j)
        no_prev = first if j == 0 else None
        return (scores(qa, j * WINDOW, kvw[:, 0:LANES], sink_a, no_prev),
                scores(qb, j * WINDOW, kvw[:, LANES:2 * LANES], sink_b, no_prev))

    def block_output(j, pm):
        kvw = kv_window(j)
        (p_a, m_a), (p_b, m_b) = pm
        o_a = weighted(p_a, m_a, kvw[:, 2 * LANES:3 * LANES], sink_a)
        o_b = weighted(p_b, m_b, kvw[:, 3 * LANES:4 * LANES], sink_b)
        rows = lambda o, sl: o[sl * WINDOW:(sl + 1) * WINDOW]
        y = jnp.concatenate([jnp.where(lo, rows(o_a, 0), rows(o_b, 0)),
                             jnp.where(lo, rows(o_a, 1), rows(o_b, 1)),
                             jnp.where(lo, rows(o_b, 2), rows(o_a, 2)),
                             jnp.where(lo, rows(o_b, 3), rows(o_a, 3))], axis=1)
        y = y * _rms_scale(y, ATTN_WIDTH) * ga_ref[...]
        ybuf[j * WINDOW:(j + 1) * WINDOW, 0:ATTN_WIDTH] = y.astype(jnp.bfloat16)

    n_blocks = tm // WINDOW
    mlp_pieces(MLP_PIECES_BEFORE_ATTENTION)
    pending = block_scores(0)
    for j in range(n_blocks):
        upcoming = block_scores(j + 1) if j + 1 < n_blocks else None
        mlp_pieces(MLP_PIECES_PER_BLOCK)
        block_output(j, pending)
        pending = upcoming
    x1 = x + _dot(ybuf[:, ATTN_WIDTH:], wo_ref[ATTN_WIDTH:, :])
    mlp_pieces(MLP_PIECES_BEFORE_OUT_PROJ)
    x1 = x1 + _dot(ybuf[:, :ATTN_WIDTH], wo_ref[:ATTN_WIDTH, :])
    mlp_pieces(len(pieces))

    x1_prev[...] = x1
    h_prev[...] = (x1 * _rms_scale(x1, D_MODEL) * g2_ref[...]).astype(jnp.bfloat16)


def _layer(x2, sinks, g1, w_in, gq, gk, conv_w, gc, ga, w_out, g2, w_up, w_down, seq_len):
    t = x2.shape[0]
    tm = TOKEN_TILE
    n_tiles = t // tm
    const = lambda shape: pl.BlockSpec(
        shape, lambda i, *_: (0,) * len(shape), pipeline_mode=pl.Buffered(1))
    grid_spec = pltpu.PrefetchScalarGridSpec(
        num_scalar_prefetch=1,
        grid=(n_tiles + 1,),
        in_specs=[
            pl.BlockSpec((tm, D_MODEL), lambda i, *_: (jnp.minimum(i, n_tiles - 1), 0)),
            const((1, D_MODEL)), const((1, HEAD_DIM)), const((1, HEAD_DIM)),
            pl.BlockSpec((None, CONV_K, CONV_WIDTH), lambda i, *_: (0, 0, 0),
                         pipeline_mode=pl.Buffered(1)),
            const((1, CONV_WIDTH)), const((1, ATTN_WIDTH)), const((1, D_MODEL)),
        ] + [pl.BlockSpec(memory_space=pl.ANY)] * 4,
        out_specs=pl.BlockSpec((tm, D_MODEL), lambda i, *_: (jnp.maximum(i - 1, 0), 0)),
        scratch_shapes=[
            pltpu.VMEM((tm, D_MODEL), jnp.float32),
            pltpu.VMEM((tm, D_MODEL), jnp.bfloat16),
            pltpu.VMEM((WINDOW, 4 * LANES), jnp.bfloat16),
            pltpu.VMEM((SUBLANES, CONV_WIDTH), jnp.float32),
            pltpu.VMEM((tm, D_MODEL), jnp.bfloat16),
            pltpu.VMEM((D_MODEL, IN_COLS), jnp.bfloat16),
            pltpu.VMEM((D_MODEL, D_MODEL), jnp.bfloat16),
            pltpu.VMEM((D_MODEL, D_FF), jnp.bfloat16),
            pltpu.VMEM((D_FF, D_MODEL), jnp.bfloat16),
            pltpu.VMEM((WEIGHT_STAGE_SLOTS,) + WEIGHT_STAGE_BLOCK, jnp.float32),
            pltpu.SemaphoreType.DMA((WEIGHT_STAGE_SLOTS,)),
        ],
    )
    return pl.pallas_call(
        functools.partial(_layer_kernel, seq_len // tm, n_tiles),
        out_shape=jax.ShapeDtypeStruct((t, D_MODEL), jnp.float32),
        grid_spec=grid_spec,
        compiler_params=pltpu.CompilerParams(
            dimension_semantics=("arbitrary",), vmem_limit_bytes=VMEM_LIMIT_BYTES),
        name="layer",
    )(sinks, x2, g1, gq, gk, conv_w, gc, ga, g2, w_in, w_out, w_up, w_down)


def kernel(x, attn_norm_g, w_in, q_norm_g, k_norm_g, sinks, conv_w, attn_out_g, conv_out_g,
           w_out, mlp_norm_g, w_up, w_down):
    b, s, d = x.shape
    depth = w_in.shape[0]
    assert d == D_MODEL and s % TOKEN_TILE == 0
    x2 = x.reshape(b * s, d)
    for l in range(depth):
        row = lambda p: p[l:l + 1]
        x2 = _layer(x2, sinks[l], row(attn_norm_g), w_in[l], row(q_norm_g), row(k_norm_g),
                    row(conv_w), row(conv_out_g), row(attn_out_g), w_out[l], row(mlp_norm_g),
                    w_up[l], w_down[l], s)
    return x2.reshape(b, s, d)
```

```python
import functools
import math

import jax
import jax.numpy as jnp
from jax import lax
from jax.experimental import pallas as pl
from jax.experimental.pallas import tpu as pltpu

D_MODEL = 1024
N_HEADS = 8
HEAD_DIM = 64
N_KV_HEADS = 2
ATTN_WIDTH = N_HEADS * HEAD_DIM
KV_WIDTH = N_KV_HEADS * HEAD_DIM
CONV_WIDTH = D_MODEL - ATTN_WIDTH
CONV_K = 3
WINDOW = 128
IN_COLS = ATTN_WIDTH + 2 * KV_WIDTH + 3 * CONV_WIDTH
D_FF = 4 * D_MODEL
EPS = 1e-6
NEG_INF = -1e30
LOG2E = math.log2(math.e)

LANES = 128
SUBLANES = 8
TOKEN_TILE = 512
FF_CHUNK = 1024
MXU_COLS = 256
MLP_PIECES_BEFORE_ATTENTION = 2
MLP_PIECES_PER_BLOCK = 2
MLP_PIECES_BEFORE_OUT_PROJ = 2
WEIGHT_STAGE_BLOCK = (512, 1024)
WEIGHT_STAGE_SLOTS = 4
VMEM_LIMIT_BYTES = 56 * 1024 * 1024

GROUP_A_HEADS = (0, 2, 5, 7)
GROUP_B_HEADS = (1, 3, 4, 6)


def _rms_scale(xf, width):
    return lax.rsqrt(jnp.sum(xf * xf, axis=-1, keepdims=True) * (1.0 / width) + EPS)


def _dot(a, b):
    return jnp.dot(a, b, preferred_element_type=jnp.float32)


def _head_mean_square(z):
    lo = lax.broadcasted_iota(jnp.int32, (z.shape[0], LANES), 1) < HEAD_DIM
    out = []
    for s in range(z.shape[1] // LANES):
        sq = z[:, s * LANES:(s + 1) * LANES]
        sq = sq * sq
        lo_sum = jnp.sum(jnp.where(lo, sq, 0.0), axis=-1, keepdims=True)
        hi_sum = jnp.sum(jnp.where(lo, 0.0, sq), axis=-1, keepdims=True)
        out.append(jnp.where(lo, lo_sum, hi_sum) * (1.0 / HEAD_DIM))
    return jnp.concatenate(out, axis=1) if len(out) > 1 else out[0]


def _weight_jobs(hbm_refs, vmem_refs, stage, sem):
    slots, rows, cols = stage.shape
    jobs = [(src, dst, r0, c0, min(cols, src.shape[1] - c0))
            for src, dst in zip(hbm_refs, vmem_refs)
            for r0 in range(0, src.shape[0], rows) for c0 in range(0, src.shape[1], cols)]

    def copy(j):
        src, _, r0, c0, nc = jobs[j]
        return pltpu.make_async_copy(src.at[pl.ds(r0, rows), pl.ds(c0, nc)],
                                     stage.at[j % slots, :, pl.ds(0, nc)], sem.at[j % slots])

    def run(j):
        _, dst, r0, c0, nc = jobs[j]
        copy(j).wait()
        dst[r0:r0 + rows, c0:c0 + nc] = stage[j % slots, :, 0:nc].astype(jnp.bfloat16)
        if j + slots < len(jobs):
            copy(j + slots).start()

    for j in range(min(slots, len(jobs))):
        copy(j).start()
    return [functools.partial(run, j) for j in range(len(jobs))]


def _layer_kernel(tiles_per_seq, n_tiles, sinks_ref, x_ref, g1_ref, gq_ref, gk_ref, cw_ref, gc_ref,
                  ga_ref, g2_ref, win_hbm, wo_hbm, wu_hbm, wd_hbm, o_ref,
                  x1_prev, h_prev, kv_tail, u_tail, ybuf, win_ref, wo_ref, wu_ref, wd_ref,
                  stage, sem):
    step = pl.program_id(0)
    refs = (sinks_ref, x_ref, g1_ref, win_ref, gq_ref, gk_ref, cw_ref, gc_ref,
            ga_ref, wo_ref, g2_ref, wu_ref, wd_ref, o_ref, x1_prev, h_prev, kv_tail, u_tail, ybuf)

    @pl.when(step == 0)
    def _():
        kv_tail[...] = jnp.zeros_like(kv_tail)
        u_tail[...] = jnp.zeros_like(u_tail)
        for job in _weight_jobs((win_hbm, wo_hbm), (win_ref, wo_ref), stage, sem):
            job()
        _step(True, _weight_jobs((wu_hbm, wd_hbm), (wu_ref, wd_ref), stage, sem), True, *refs)

    @pl.when(jnp.logical_and(step > 0, step < n_tiles))
    def _():
        _step(True, None, (step % tiles_per_seq) == 0, *refs)

    @pl.when(step == n_tiles)
    def _():
        _step(False, None, False, *refs)


def _step(with_mixer, fillers, first, sinks_ref, x_ref, g1_ref, win_ref, gq_ref, gk_ref,
          cw_ref, gc_ref, ga_ref, wo_ref, g2_ref, wu_ref, wd_ref,
          o_ref, x1_prev, h_prev, kv_tail, u_tail, ybuf):
    tm = TOKEN_TILE
    with_mlp = fillers is None

    if with_mlp:
        o_ref[...] = x1_prev[...]
    n_up = FF_CHUNK // MXU_COLS
    n_down = D_MODEL // MXU_COLS
    acts = {}

    def up_piece(c, n):
        w0 = c * FF_CHUNK + n * MXU_COLS
        a = jnp.maximum(_dot(h_prev[...], wu_ref[:, w0:w0 + MXU_COLS]), 0.0)
        acts[c, n] = (a * a).astype(jnp.bfloat16)

    def down_piece(c, n):
        a = jnp.concatenate([acts[c, i] for i in range(n_up)], axis=1)
        o_ref[:, n * MXU_COLS:(n + 1) * MXU_COLS] += _dot(
            a, wd_ref[c * FF_CHUNK:(c + 1) * FF_CHUNK, n * MXU_COLS:(n + 1) * MXU_COLS])

    pieces = list(fillers) if fillers is not None else [
        functools.partial(f, c, n) for c in range(D_FF // FF_CHUNK)
        for f, count in ((up_piece, n_up), (down_piece, n_down)) for n in range(count)]

    def mlp_pieces(count):
        for _ in range(min(count, len(pieces))):
            pieces.pop(0)()

    if not with_mixer:
        mlp_pieces(len(pieces))
        return
    mlp_pieces(4)

    x = x_ref[...]
    h = (x * _rms_scale(x, D_MODEL) * g1_ref[...]).astype(jnp.bfloat16)
    o0 = ATTN_WIDTH
    o1 = o0 + 2 * KV_WIDTH
    zq = _dot(h, win_ref[:, :o0])
    zkv = _dot(h, win_ref[:, o0:o1])
    k = zkv[:, :KV_WIDTH]
    v = zkv[:, KV_WIDTH:]
    half = ATTN_WIDTH // 2
    q_ms = _head_mean_square(zq)
    k_ms = _head_mean_square(k)
    zc = _dot(h, win_ref[:, o1:])
    gq = jnp.tile(gq_ref[...], (1, N_HEADS)) * (HEAD_DIM ** -0.5 * LOG2E)
    gk = jnp.tile(gk_ref[...], (1, N_KV_HEADS))
    qn = zq * lax.rsqrt(q_ms + EPS) * gq
    kn = k * lax.rsqrt(k_ms + EPS) * gk
    col = lax.broadcasted_iota(jnp.int32, (tm, ATTN_WIDTH), 1)
    in_a = ((col % LANES) < HEAD_DIM) == (col < half)
    qa = jnp.where(in_a, qn, 0.0).astype(jnp.bfloat16)
    qb = jnp.where(in_a, 0.0, qn).astype(jnp.bfloat16)
    bf = lambda t: t.astype(jnp.bfloat16)
    kv = jnp.concatenate([bf(kn), bf(pltpu.roll(kn, HEAD_DIM, 1)),
                          bf(v), bf(pltpu.roll(v, HEAD_DIM, 1))], axis=1)
    kv_prev = kv_tail[...]
    kv_tail[...] = kv[tm - WINDOW:]

    u = zc[:, CONV_WIDTH:2 * CONV_WIDTH] * zc[:, 2 * CONV_WIDTH:]
    ext = jnp.concatenate([jnp.where(first, 0.0, u_tail[...]), u], axis=0)
    u1 = pltpu.roll(ext, 1, 0)[SUBLANES:]
    u2 = pltpu.roll(ext, 2, 0)[SUBLANES:]
    u_tail[...] = u[tm - SUBLANES:]
    conv = cw_ref[0:1, :] * u2 + cw_ref[1:2, :] * u1 + cw_ref[2:3, :] * u
    yc = zc[:, :CONV_WIDTH] * conv
    ybuf[:, ATTN_WIDTH:] = (yc * _rms_scale(yc, CONV_WIDTH) * gc_ref[...]).astype(jnp.bfloat16)

    lane = lax.broadcasted_iota(jnp.int32, (WINDOW, LANES), 1)
    lo = lane < HEAD_DIM
    qi = lax.broadcasted_iota(jnp.int32, (WINDOW, WINDOW), 0)
    kj = lax.broadcasted_iota(jnp.int32, (WINDOW, WINDOW), 1)
    from_prev = jnp.concatenate([kj > qi] * 4, axis=0)

    def sink_rows(heads):
        return jnp.concatenate(
            [jnp.full((WINDOW, LANES), sinks_ref[hd] * LOG2E, jnp.float32) for hd in heads], axis=0)

    sink_a = sink_rows(GROUP_A_HEADS)
    sink_b = sink_rows(GROUP_B_HEADS)
    ones = jnp.ones((2 * WINDOW, LANES), jnp.bfloat16)

    def scores(qg, r0, kk, sink, no_prev):
        qs = jnp.concatenate(
            [qg[r0:r0 + WINDOW, sl * LANES:(sl + 1) * LANES] for sl in range(4)], axis=0)
        s = lax.dot_general(qs, kk, (((1,), (1,)), ((), ())),
                            preferred_element_type=jnp.float32)
        s_prev = s[:, :LANES] if no_prev is None else jnp.where(no_prev, NEG_INF, s[:, :LANES])
        s = jnp.where(from_prev, s_prev, s[:, LANES:])
        m = jnp.max(s, axis=-1, keepdims=True)
        m = jnp.maximum(jnp.broadcast_to(m, sink.shape), sink)
        return jnp.exp2(s - m).astype(jnp.bfloat16), m

    def weighted(p, m, vv, sink):
        zero = jnp.zeros_like(p)
        p = jnp.concatenate([jnp.where(from_prev, p, zero), jnp.where(from_prev, zero, p)],
                            axis=1)
        ov = _dot(p, jnp.concatenate([vv, ones], axis=1))
        return ov[:, :LANES], ov[:, LANES:] + jnp.exp2(sink - m)

    def kv_window(j):
        if j == 0:
            return jnp.concatenate([kv_prev, kv[0:WINDOW]], axis=0)
        return kv[(j - 1) * WINDOW:(j + 1) * WINDOW]

    def block_scores(j):
        kvw = kv_window(j)
        no_prev = first if j == 0 else None
        return (scores(qa, j * WINDOW, kvw[:, 0:LANES], sink_a, no_prev),
                scores(qb, j * WINDOW, kvw[:, LANES:2 * LANES], sink_b, no_prev))

    def block_output(j, pm):
        kvw = kv_window(j)
        (p_a, m_a), (p_b, m_b) = pm
        o_a, l_a = weighted(p_a, m_a, kvw[:, 2 * LANES:3 * LANES], sink_a)
        o_b, l_b = weighted(p_b, m_b, kvw[:, 3 * LANES:4 * LANES], sink_b)
        rows = lambda o, sl: o[sl * WINDOW:(sl + 1) * WINDOW]

        def merge(t_a, t_b):
            return jnp.concatenate([jnp.where(lo, rows(t_a, 0), rows(t_b, 0)),
                                    jnp.where(lo, rows(t_a, 1), rows(t_b, 1)),
                                    jnp.where(lo, rows(t_b, 2), rows(t_a, 2)),
                                    jnp.where(lo, rows(t_b, 3), rows(t_a, 3))], axis=1)

        y = merge(o_a, o_b) / merge(l_a, l_b)
        y = y * _rms_scale(y, ATTN_WIDTH) * ga_ref[...]
        ybuf[j * WINDOW:(j + 1) * WINDOW, 0:ATTN_WIDTH] = y.astype(jnp.bfloat16)

    n_blocks = tm // WINDOW
    mlp_pieces(MLP_PIECES_BEFORE_ATTENTION)
    pending = block_scores(0)
    for j in range(n_blocks):
        upcoming = block_scores(j + 1) if j + 1 < n_blocks else None
        mlp_pieces(MLP_PIECES_PER_BLOCK)
        block_output(j, pending)
        pending = upcoming
    x1 = x + _dot(ybuf[:, ATTN_WIDTH:], wo_ref[ATTN_WIDTH:, :])
    mlp_pieces(MLP_PIECES_BEFORE_OUT_PROJ)
    x1 = x1 + _dot(ybuf[:, :ATTN_WIDTH], wo_ref[:ATTN_WIDTH, :])
    mlp_pieces(len(pieces))

    x1_prev[...] = x1
    h_prev[...] = (x1 * _rms_scale(x1, D_MODEL) * g2_ref[...]).astype(jnp.bfloat16)


def _layer(x2, sinks, g1, w_in, gq, gk, conv_w, gc, ga, w_out, g2, w_up, w_down, seq_len):
    t = x2.shape[0]
    tm = TOKEN_TILE
    n_tiles = t // tm
    const = lambda shape: pl.BlockSpec(
        shape, lambda i, *_: (0,) * len(shape), pipeline_mode=pl.Buffered(1))
    grid_spec = pltpu.PrefetchScalarGridSpec(
        num_scalar_prefetch=1,
        grid=(n_tiles + 1,),
        in_specs=[
            pl.BlockSpec((tm, D_MODEL), lambda i, *_: (jnp.minimum(i, n_tiles - 1), 0)),
            const((1, D_MODEL)), const((1, HEAD_DIM)), const((1, HEAD_DIM)),
            pl.BlockSpec((None, CONV_K, CONV_WIDTH), lambda i, *_: (0, 0, 0),
                         pipeline_mode=pl.Buffered(1)),
            const((1, CONV_WIDTH)), const((1, ATTN_WIDTH)), const((1, D_MODEL)),
        ] + [pl.BlockSpec(memory_space=pl.ANY)] * 4,
        out_specs=pl.BlockSpec((tm, D_MODEL), lambda i, *_: (jnp.maximum(i - 1, 0), 0)),
        scratch_shapes=[
            pltpu.VMEM((tm, D_MODEL), jnp.float32),
            pltpu.VMEM((tm, D_MODEL), jnp.bfloat16),
            pltpu.VMEM((WINDOW, 4 * LANES), jnp.bfloat16),
            pltpu.VMEM((SUBLANES, CONV_WIDTH), jnp.float32),
            pltpu.VMEM((tm, D_MODEL), jnp.bfloat16),
            pltpu.VMEM((D_MODEL, IN_COLS), jnp.bfloat16),
            pltpu.VMEM((D_MODEL, D_MODEL), jnp.bfloat16),
            pltpu.VMEM((D_MODEL, D_FF), jnp.bfloat16),
            pltpu.VMEM((D_FF, D_MODEL), jnp.bfloat16),
            pltpu.VMEM((WEIGHT_STAGE_SLOTS,) + WEIGHT_STAGE_BLOCK, jnp.float32),
            pltpu.SemaphoreType.DMA((WEIGHT_STAGE_SLOTS,)),
        ],
    )
    return pl.pallas_call(
        functools.partial(_layer_kernel, seq_len // tm, n_tiles),
        out_shape=jax.ShapeDtypeStruct((t, D_MODEL), jnp.float32),
        grid_spec=grid_spec,
        compiler_params=pltpu.CompilerParams(
            dimension_semantics=("arbitrary",), vmem_limit_bytes=VMEM_LIMIT_BYTES),
        name="layer",
    )(sinks, x2, g1, gq, gk, conv_w, gc, ga, g2, w_in, w_out, w_up, w_down)


def kernel(x, attn_norm_g, w_in, q_norm_g, k_norm_g, sinks, conv_w, attn_out_g, conv_out_g,
           w_out, mlp_norm_g, w_up, w_down):
    b, s, d = x.shape
    depth = w_in.shape[0]
    assert d == D_MODEL and s % TOKEN_TILE == 0
    x2 = x.reshape(b * s, d)
    for l in range(depth):
        row = lambda p: p[l:l + 1]
        x2 = _layer(x2, sinks[l], row(attn_norm_g), w_in[l], row(q_norm_g), row(k_norm_g),
                    row(conv_w), row(conv_out_g), row(attn_out_g), w_out[l], row(mlp_norm_g),
                    w_up[l], w_down[l], s)
    return x2.reshape(b, s, d)
```

```python
import functools
import math

import jax
import jax.numpy as jnp
from jax import lax
from jax.experimental import pallas as pl
from jax.experimental.pallas import tpu as pltpu

D_MODEL = 1024
N_HEADS = 8
HEAD_DIM = 64
N_KV_HEADS = 2
ATTN_WIDTH = N_HEADS * HEAD_DIM
KV_WIDTH = N_KV_HEADS * HEAD_DIM
CONV_WIDTH = D_MODEL - ATTN_WIDTH
CONV_K = 3
WINDOW = 128
IN_COLS = ATTN_WIDTH + 2 * KV_WIDTH + 3 * CONV_WIDTH
D_FF = 4 * D_MODEL
EPS = 1e-6
NEG_INF = -1e30
LOG2E = math.log2(math.e)

LANES = 128
SUBLANES = 8
TOKEN_TILE = 512
FF_CHUNK = 1024
MXU_COLS = 256
MLP_PIECES_BEFORE_ATTENTION = 2
MLP_PIECES_PER_BLOCK = 2
MLP_PIECES_BEFORE_OUT_PROJ = 2
WEIGHT_STAGE_BLOCK = (512, 1024)
WEIGHT_STAGE_SLOTS = 4
VMEM_LIMIT_BYTES = 56 * 1024 * 1024

GROUP_A_HEADS = (0, 2, 5, 7)
GROUP_B_HEADS = (1, 3, 4, 6)


def _rms_scale(xf, width):
    return lax.rsqrt(jnp.sum(xf * xf, axis=-1, keepdims=True) * (1.0 / width) + EPS)


def _dot(a, b):
    return jnp.dot(a, b, preferred_element_type=jnp.float32)


def _head_mean_square(z):
    lo = lax.broadcasted_iota(jnp.int32, (z.shape[0], LANES), 1) < HEAD_DIM
    out = []
    for s in range(z.shape[1] // LANES):
        sq = z[:, s * LANES:(s + 1) * LANES]
        sq = sq * sq
        lo_sum = jnp.sum(jnp.where(lo, sq, 0.0), axis=-1, keepdims=True)
        hi_sum = jnp.sum(jnp.where(lo, 0.0, sq), axis=-1, keepdims=True)
        out.append(jnp.where(lo, lo_sum, hi_sum) * (1.0 / HEAD_DIM))
    return jnp.concatenate(out, axis=1) if len(out) > 1 else out[0]


def _weight_jobs(hbm_refs, vmem_refs, stage, sem):
    slots, rows, cols = stage.shape
    jobs = [(src, dst, r0, c0, min(cols, src.shape[1] - c0))
            for src, dst in zip(hbm_refs, vmem_refs)
            for r0 in range(0, src.shape[0], rows) for c0 in range(0, src.shape[1], cols)]

    def copy(j):
        src, _, r0, c0, nc = jobs[j]
        return pltpu.make_async_copy(src.at[pl.ds(r0, rows), pl.ds(c0, nc)],
                                     stage.at[j % slots, :, pl.ds(0, nc)], sem.at[j % slots])

    def run(j):
        _, dst, r0, c0, nc = jobs[j]
        copy(j).wait()
        dst[r0:r0 + rows, c0:c0 + nc] = stage[j % slots, :, 0:nc].astype(jnp.bfloat16)
        if j + slots < len(jobs):
            copy(j + slots).start(priority=(j + slots) % 2)

    for j in range(min(slots, len(jobs))):
        copy(j).start(priority=j % 2)
    return [functools.partial(run, j) for j in range(len(jobs))]


def _layer_kernel(tiles_per_seq, n_tiles, sinks_ref, x_ref, g1_ref, gq_ref, gk_ref, cw_ref, gc_ref,
                  ga_ref, g2_ref, win_hbm, wo_hbm, wu_hbm, wd_hbm, o_ref,
                  x1_prev, h_prev, kv_tail, u_tail, ybuf, win_ref, wo_ref, wu_ref, wd_ref,
                  stage, sem):
    step = pl.program_id(0)
    refs = (sinks_ref, x_ref, g1_ref, win_ref, gq_ref, gk_ref, cw_ref, gc_ref,
            ga_ref, wo_ref, g2_ref, wu_ref, wd_ref, o_ref, x1_prev, h_prev, kv_tail, u_tail, ybuf)

    @pl.when(step == 0)
    def _():
        kv_tail[...] = jnp.zeros_like(kv_tail)
        u_tail[...] = jnp.zeros_like(u_tail)
        for job in _weight_jobs((win_hbm, wo_hbm), (win_ref, wo_ref), stage, sem):
            job()
        _step(True, _weight_jobs((wu_hbm, wd_hbm), (wu_ref, wd_ref), stage, sem), True, *refs)

    @pl.when(jnp.logical_and(step > 0, step < n_tiles))
    def _():
        _step(True, None, (step % tiles_per_seq) == 0, *refs)

    @pl.when(step == n_tiles)
    def _():
        _step(False, None, False, *refs)


def _step(with_mixer, fillers, first, sinks_ref, x_ref, g1_ref, win_ref, gq_ref, gk_ref,
          cw_ref, gc_ref, ga_ref, wo_ref, g2_ref, wu_ref, wd_ref,
          o_ref, x1_prev, h_prev, kv_tail, u_tail, ybuf):
    tm = TOKEN_TILE
    with_mlp = fillers is None

    if with_mlp:
        o_ref[...] = x1_prev[...]
    n_up = FF_CHUNK // MXU_COLS
    n_down = D_MODEL // MXU_COLS
    acts = {}

    def up_piece(c, n):
        w0 = c * FF_CHUNK + n * MXU_COLS
        a = jnp.maximum(_dot(h_prev[...], wu_ref[:, w0:w0 + MXU_COLS]), 0.0)
        acts[c, n] = (a * a).astype(jnp.bfloat16)

    def down_piece(c, n):
        a = jnp.concatenate([acts[c, i] for i in range(n_up)], axis=1)
        o_ref[:, n * MXU_COLS:(n + 1) * MXU_COLS] += _dot(
            a, wd_ref[c * FF_CHUNK:(c + 1) * FF_CHUNK, n * MXU_COLS:(n + 1) * MXU_COLS])

    pieces = list(fillers) if fillers is not None else [
        functools.partial(f, c, n) for c in range(D_FF // FF_CHUNK)
        for f, count in ((up_piece, n_up), (down_piece, n_down)) for n in range(count)]

    def mlp_pieces(count):
        for _ in range(min(count, len(pieces))):
            pieces.pop(0)()

    if not with_mixer:
        mlp_pieces(len(pieces))
        return
    mlp_pieces(4)

    x = x_ref[...]
    h = (x * _rms_scale(x, D_MODEL) * g1_ref[...]).astype(jnp.bfloat16)
    o0 = ATTN_WIDTH
    o1 = o0 + 2 * KV_WIDTH
    zq = _dot(h, win_ref[:, :o0])
    zkv = _dot(h, win_ref[:, o0:o1])
    k = zkv[:, :KV_WIDTH]
    v = zkv[:, KV_WIDTH:]
    half = ATTN_WIDTH // 2
    q_ms = _head_mean_square(zq)
    k_ms = _head_mean_square(k)
    zc = _dot(h, win_ref[:, o1:])
    gq = jnp.tile(gq_ref[...], (1, N_HEADS)) * (HEAD_DIM ** -0.5 * LOG2E)
    gk = jnp.tile(gk_ref[...], (1, N_KV_HEADS))
    qn = zq * lax.rsqrt(q_ms + EPS) * gq
    kn = k * lax.rsqrt(k_ms + EPS) * gk
    col = lax.broadcasted_iota(jnp.int32, (tm, ATTN_WIDTH), 1)
    in_a = ((col % LANES) < HEAD_DIM) == (col < half)
    qa = jnp.where(in_a, qn, 0.0).astype(jnp.bfloat16)
    qb = jnp.where(in_a, 0.0, qn).astype(jnp.bfloat16)
    bf = lambda t: t.astype(jnp.bfloat16)
    kv = jnp.concatenate([bf(kn), bf(pltpu.roll(kn, HEAD_DIM, 1)),
                          bf(v), bf(pltpu.roll(v, HEAD_DIM, 1))], axis=1)
    kv_prev = kv_tail[...]
    kv_tail[...] = kv[tm - WINDOW:]

    u = zc[:, CONV_WIDTH:2 * CONV_WIDTH] * zc[:, 2 * CONV_WIDTH:]
    ext = jnp.concatenate([jnp.where(first, 0.0, u_tail[...]), u], axis=0)
    u1 = pltpu.roll(ext, 1, 0)[SUBLANES:]
    u2 = pltpu.roll(ext, 2, 0)[SUBLANES:]
    u_tail[...] = u[tm - SUBLANES:]
    conv = cw_ref[0:1, :] * u2 + cw_ref[1:2, :] * u1 + cw_ref[2:3, :] * u
    yc = zc[:, :CONV_WIDTH] * conv
    ybuf[:, ATTN_WIDTH:] = (yc * _rms_scale(yc, CONV_WIDTH) * gc_ref[...]).astype(jnp.bfloat16)

    lane = lax.broadcasted_iota(jnp.int32, (WINDOW, LANES), 1)
    lo = lane < HEAD_DIM
    qi = lax.broadcasted_iota(jnp.int32, (WINDOW, WINDOW), 0)
    kj = lax.broadcasted_iota(jnp.int32, (WINDOW, WINDOW), 1)
    from_prev = jnp.concatenate([kj > qi] * 4, axis=0)

    def sink_rows(heads):
        return jnp.concatenate(
            [jnp.full((WINDOW, LANES), sinks_ref[hd] * LOG2E, jnp.float32) for hd in heads], axis=0)

    sink_a = sink_rows(GROUP_A_HEADS)
    sink_b = sink_rows(GROUP_B_HEADS)
    ones = jnp.ones((2 * WINDOW, LANES), jnp.bfloat16)

    def scores(qg, r0, kk, sink, no_prev):
        qs = jnp.concatenate(
            [qg[r0:r0 + WINDOW, sl * LANES:(sl + 1) * LANES] for sl in range(4)], axis=0)
        s = lax.dot_general(qs, kk, (((1,), (1,)), ((), ())),
                            preferred_element_type=jnp.float32)
        s_prev = s[:, :LANES] if no_prev is None else jnp.where(no_prev, NEG_INF, s[:, :LANES])
        s = jnp.where(from_prev, s_prev, s[:, LANES:])
        m = jnp.max(s, axis=-1, keepdims=True)
        m = jnp.maximum(jnp.broadcast_to(m, sink.shape), sink)
        return jnp.exp2(s - m), m

    def weighted(p, m, vv, sink):
        p = jnp.concatenate([jnp.where(from_prev, p, 0.0), jnp.where(from_prev, 0.0, p)],
                            axis=1).astype(jnp.bfloat16)
        ov = _dot(p, jnp.concatenate([vv, ones], axis=1))
        l = ov[:, LANES:] + jnp.exp2(sink - m)
        return ov[:, :LANES] / l

    def kv_window(j):
        if j == 0:
            return jnp.concatenate([kv_prev, kv[0:WINDOW]], axis=0)
        return kv[(j - 1) * WINDOW:(j + 1) * WINDOW]

    def block_scores(j):
        kvw = kv_window(j)
        no_prev = first if j == 0 else None
        return (scores(qa, j * WINDOW, kvw[:, 0:LANES], sink_a, no_prev),
                scores(qb, j * WINDOW, kvw[:, LANES:2 * LANES], sink_b, no_prev))

    def block_output(j, pm):
        kvw = kv_window(j)
        (p_a, m_a), (p_b, m_b) = pm
        o_a = weighted(p_a, m_a, kvw[:, 2 * LANES:3 * LANES], sink_a)
        o_b = weighted(p_b, m_b, kvw[:, 3 * LANES:4 * LANES], sink_b)
        rows = lambda o, sl: o[sl * WINDOW:(sl + 1) * WINDOW]
        y = jnp.concatenate([jnp.where(lo, rows(o_a, 0), rows(o_b, 0)),
                             jnp.where(lo, rows(o_a, 1), rows(o_b, 1)),
                             jnp.where(lo, rows(o_b, 2), rows(o_a, 2)),
                             jnp.where(lo, rows(o_b, 3), rows(o_a, 3))], axis=1)
        y = y * _rms_scale(y, ATTN_WIDTH) * ga_ref[...]
        ybuf[j * WINDOW:(j + 1) * WINDOW, 0:ATTN_WIDTH] = y.astype(jnp.bfloat16)

    n_blocks = tm // WINDOW
    mlp_pieces(MLP_PIECES_BEFORE_ATTENTION)
    pending = block_scores(0)
    for j in range(n_blocks):
        upcoming = block_scores(j + 1) if j + 1 < n_blocks else None
        mlp_pieces(MLP_PIECES_PER_BLOCK)
        block_output(j, pending)
        pending = upcoming
    x1 = x + _dot(ybuf[:, ATTN_WIDTH:], wo_ref[ATTN_WIDTH:, :])
    mlp_pieces(MLP_PIECES_BEFORE_OUT_PROJ)
    x1 = x1 + _dot(ybuf[:, :ATTN_WIDTH], wo_ref[:ATTN_WIDTH, :])
    mlp_pieces(len(pieces))

    x1_prev[...] = x1
    h_prev[...] = (x1 * _rms_scale(x1, D_MODEL) * g2_ref[...]).astype(jnp.bfloat16)


def _layer(x2, sinks, g1, w_in, gq, gk, conv_w, gc, ga, w_out, g2, w_up, w_down, seq_len):
    t = x2.shape[0]
    tm = TOKEN_TILE
    n_tiles = t // tm
    const = lambda shape: pl.BlockSpec(
        shape, lambda i, *_: (0,) * len(shape), pipeline_mode=pl.Buffered(1))
    grid_spec = pltpu.PrefetchScalarGridSpec(
        num_scalar_prefetch=1,
        grid=(n_tiles + 1,),
        in_specs=[
            pl.BlockSpec((tm, D_MODEL), lambda i, *_: (jnp.minimum(i, n_tiles - 1), 0)),
            const((1, D_MODEL)), const((1, HEAD_DIM)), const((1, HEAD_DIM)),
            pl.BlockSpec((None, CONV_K, CONV_WIDTH), lambda i, *_: (0, 0, 0),
                         pipeline_mode=pl.Buffered(1)),
            const((1, CONV_WIDTH)), const((1, ATTN_WIDTH)), const((1, D_MODEL)),
        ] + [pl.BlockSpec(memory_space=pl.ANY)] * 4,
        out_specs=pl.BlockSpec((tm, D_MODEL), lambda i, *_: (jnp.maximum(i - 1, 0), 0)),
        scratch_shapes=[
            pltpu.VMEM((tm, D_MODEL), jnp.float32),
            pltpu.VMEM((tm, D_MODEL), jnp.bfloat16),
            pltpu.VMEM((WINDOW, 4 * LANES), jnp.bfloat16),
            pltpu.VMEM((SUBLANES, CONV_WIDTH), jnp.float32),
            pltpu.VMEM((tm, D_MODEL), jnp.bfloat16),
            pltpu.VMEM((D_MODEL, IN_COLS), jnp.bfloat16),
            pltpu.VMEM((D_MODEL, D_MODEL), jnp.bfloat16),
            pltpu.VMEM((D_MODEL, D_FF), jnp.bfloat16),
            pltpu.VMEM((D_FF, D_MODEL), jnp.bfloat16),
            pltpu.VMEM((WEIGHT_STAGE_SLOTS,) + WEIGHT_STAGE_BLOCK, jnp.float32),
            pltpu.SemaphoreType.DMA((WEIGHT_STAGE_SLOTS,)),
        ],
    )
    return pl.pallas_call(
        functools.partial(_layer_kernel, seq_len // tm, n_tiles),
        out_shape=jax.ShapeDtypeStruct((t, D_MODEL), jnp.float32),
        grid_spec=grid_spec,
        compiler_params=pltpu.CompilerParams(
            dimension_semantics=("arbitrary",), vmem_limit_bytes=VMEM_LIMIT_BYTES),
        name="layer",
    )(sinks, x2, g1, gq, gk, conv_w, gc, ga, g2, w_in, w_out, w_up, w_down)


def kernel(x, attn_norm_g, w_in, q_norm_g, k_norm_g, sinks, conv_w, attn_out_g, conv_out_g,
           w_out, mlp_norm_g, w_up, w_down):
    b, s, d = x.shape
    depth = w_in.shape[0]
    assert d == D_MODEL and s % TOKEN_TILE == 0
    x2 = x.reshape(b * s, d)
    for l in range(depth):
        row = lambda p: p[l:l + 1]
        x2 = _layer(x2, sinks[l], row(attn_norm_g), w_in[l], row(q_norm_g), row(k_norm_g),
                    row(conv_w), row(conv_out_g), row(attn_out_g), w_out[l], row(mlp_norm_g),
                    w_up[l], w_down[l], s)
    return x2.reshape(b, s, d)
```

```python
import functools
import math

import jax
import jax.numpy as jnp
from jax import lax
from jax.experimental import pallas as pl
from jax.experimental.pallas import tpu as pltpu

D_MODEL = 1024
N_HEADS = 8
HEAD_DIM = 64
N_KV_HEADS = 2
ATTN_WIDTH = N_HEADS * HEAD_DIM
KV_WIDTH = N_KV_HEADS * HEAD_DIM
CONV_WIDTH = D_MODEL - ATTN_WIDTH
CONV_K = 3
WINDOW = 128
IN_COLS = ATTN_WIDTH + 2 * KV_WIDTH + 3 * CONV_WIDTH
D_FF = 4 * D_MODEL
EPS = 1e-6
NEG_INF = -1e30
LOG2E = math.log2(math.e)

LANES = 128
SUBLANES = 8
TOKEN_TILE = 512
FF_CHUNK = 1024
MXU_COLS = 256
MLP_PIECES_BEFORE_ATTENTION = 2
MLP_PIECES_PER_BLOCK = 2
MLP_PIECES_BEFORE_OUT_PROJ = 2
WEIGHT_STAGE_BLOCK = (512, 1024)
WEIGHT_STAGE_SLOTS = 4
VMEM_LIMIT_BYTES = 56 * 1024 * 1024

GROUP_A_HEADS = (0, 2, 5, 7)
GROUP_B_HEADS = (1, 3, 4, 6)


def _rms_scale(xf, width):
    return lax.rsqrt(jnp.sum(xf * xf, axis=-1, keepdims=True) * (1.0 / width) + EPS)


def _dot(a, b):
    return jnp.dot(a, b, preferred_element_type=jnp.float32)


def _head_mean_square(z):
    lo = lax.broadcasted_iota(jnp.int32, (z.shape[0], LANES), 1) < HEAD_DIM
    out = []
    for s in range(z.shape[1] // LANES):
        sq = z[:, s * LANES:(s + 1) * LANES]
        sq = sq * sq
        lo_sum = jnp.sum(jnp.where(lo, sq, 0.0), axis=-1, keepdims=True)
        hi_sum = jnp.sum(jnp.where(lo, 0.0, sq), axis=-1, keepdims=True)
        out.append(jnp.where(lo, lo_sum, hi_sum) * (1.0 / HEAD_DIM))
    return jnp.concatenate(out, axis=1) if len(out) > 1 else out[0]


def _weight_jobs(hbm_refs, vmem_refs, stage, sem):
    slots, rows, cols = stage.shape
    jobs = [(src, dst, r0, c0, min(cols, src.shape[1] - c0))
            for src, dst in zip(hbm_refs, vmem_refs)
            for r0 in range(0, src.shape[0], rows) for c0 in range(0, src.shape[1], cols)]

    def copy(j):
        src, _, r0, c0, nc = jobs[j]
        return pltpu.make_async_copy(src.at[pl.ds(r0, rows), pl.ds(c0, nc)],
                                     stage.at[j % slots, :, pl.ds(0, nc)], sem.at[j % slots])

    def run(j):
        _, dst, r0, c0, nc = jobs[j]
        copy(j).wait()
        dst[r0:r0 + rows, c0:c0 + nc] = stage[j % slots, :, 0:nc].astype(jnp.bfloat16)
        if j + slots < len(jobs):
            copy(j + slots).start()

    for j in range(min(slots, len(jobs))):
        copy(j).start()
    return [functools.partial(run, j) for j in range(len(jobs))]


def _layer_kernel(tiles_per_seq, n_tiles, sinks_ref, x_ref, g1_ref, gq_ref, gk_ref, cw_ref, gc_ref,
                  ga_ref, g2_ref, win_hbm, wo_hbm, wu_hbm, wd_hbm, o_ref,
                  x1_prev, h_prev, kv_tail, u_tail, ybuf, win_ref, wo_ref, wu_ref, wd_ref,
                  stage, sem):
    step = pl.program_id(0)
    refs = (sinks_ref, x_ref, g1_ref, win_ref, gq_ref, gk_ref, cw_ref, gc_ref,
            ga_ref, wo_ref, g2_ref, wu_ref, wd_ref, o_ref, x1_prev, h_prev, kv_tail, u_tail, ybuf)

    @pl.when(step == 0)
    def _():
        kv_tail[...] = jnp.zeros_like(kv_tail)
        u_tail[...] = jnp.zeros_like(u_tail)
        for job in _weight_jobs((win_hbm, wo_hbm), (win_ref, wo_ref), stage, sem):
            job()
        _step(True, _weight_jobs((wu_hbm, wd_hbm), (wu_ref, wd_ref), stage, sem), True, *refs)

    @pl.when(jnp.logical_and(step > 0, step < n_tiles))
    def _():
        _step(True, None, (step % tiles_per_seq) == 0, *refs)

    @pl.when(step == n_tiles)
    def _():
        _step(False, None, False, *refs)


def _step(with_mixer, fillers, first, sinks_ref, x_ref, g1_ref, win_ref, gq_ref, gk_ref,
          cw_ref, gc_ref, ga_ref, wo_ref, g2_ref, wu_ref, wd_ref,
          o_ref, x1_prev, h_prev, kv_tail, u_tail, ybuf):
    tm = TOKEN_TILE
    with_mlp = fillers is None

    if with_mlp:
        o_ref[...] = x1_prev[...]
    n_up = FF_CHUNK // MXU_COLS
    n_down = D_MODEL // MXU_COLS
    acts = {}

    def up_piece(c, n):
        w0 = c * FF_CHUNK + n * MXU_COLS
        a = jnp.maximum(_dot(h_prev[...], wu_ref[:, w0:w0 + MXU_COLS]), 0.0)
        acts[c, n] = (a * a).astype(jnp.bfloat16)

    def down_piece(c, n):
        a = jnp.concatenate([acts[c, i] for i in range(n_up)], axis=1)
        o_ref[:, n * MXU_COLS:(n + 1) * MXU_COLS] += _dot(
            a, wd_ref[c * FF_CHUNK:(c + 1) * FF_CHUNK, n * MXU_COLS:(n + 1) * MXU_COLS])

    pieces = list(fillers) if fillers is not None else [
        functools.partial(f, c, n) for c in range(D_FF // FF_CHUNK)
        for f, count in ((up_piece, n_up), (down_piece, n_down)) for n in range(count)]

    def mlp_pieces(count):
        for _ in range(min(count, len(pieces))):
            pieces.pop(0)()

    if not with_mixer:
        mlp_pieces(len(pieces))
        return
    mlp_pieces(4)

    x = x_ref[...]
    h = (x * _rms_scale(x, D_MODEL) * g1_ref[...]).astype(jnp.bfloat16)
    o0 = ATTN_WIDTH
    o1 = o0 + 2 * KV_WIDTH
    zq = _dot(h, win_ref[:, :o0])
    zkv = _dot(h, win_ref[:, o0:o1])
    k = zkv[:, :KV_WIDTH]
    v = zkv[:, KV_WIDTH:]
    half = ATTN_WIDTH // 2
    q_ms = _head_mean_square(zq)
    k_ms = _head_mean_square(k)
    z_gate_c = _dot(h, win_ref[:, o1 + CONV_WIDTH:o1 + 2 * CONV_WIDTH])
    z_conv_in = _dot(h, win_ref[:, o1 + 2 * CONV_WIDTH:])
    z_gate_b = _dot(h, win_ref[:, o1:o1 + CONV_WIDTH])
    gq = jnp.tile(gq_ref[...], (1, N_HEADS)) * (HEAD_DIM ** -0.5 * LOG2E)
    gk = jnp.tile(gk_ref[...], (1, N_KV_HEADS))
    qn = zq * lax.rsqrt(q_ms + EPS) * gq
    kn = k * lax.rsqrt(k_ms + EPS) * gk
    col = lax.broadcasted_iota(jnp.int32, (tm, ATTN_WIDTH), 1)
    in_a = ((col % LANES) < HEAD_DIM) == (col < half)
    qa = jnp.where(in_a, qn, 0.0).astype(jnp.bfloat16)
    qb = jnp.where(in_a, 0.0, qn).astype(jnp.bfloat16)
    bf = lambda t: t.astype(jnp.bfloat16)
    kv = jnp.concatenate([bf(kn), bf(pltpu.roll(kn, HEAD_DIM, 1)),
                          bf(v), bf(pltpu.roll(v, HEAD_DIM, 1))], axis=1)
    kv_prev = kv_tail[...]
    kv_tail[...] = kv[tm - WINDOW:]

    u = z_gate_c * z_conv_in
    ext = jnp.concatenate([jnp.where(first, 0.0, u_tail[...]), u], axis=0)
    u1 = pltpu.roll(ext, 1, 0)[SUBLANES:]
    u2 = pltpu.roll(ext, 2, 0)[SUBLANES:]
    u_tail[...] = u[tm - SUBLANES:]
    conv = cw_ref[0:1, :] * u2 + cw_ref[1:2, :] * u1 + cw_ref[2:3, :] * u
    yc = z_gate_b * conv
    ybuf[:, ATTN_WIDTH:] = (yc * _rms_scale(yc, CONV_WIDTH) * gc_ref[...]).astype(jnp.bfloat16)

    lane = lax.broadcasted_iota(jnp.int32, (WINDOW, LANES), 1)
    lo = lane < HEAD_DIM
    qi = lax.broadcasted_iota(jnp.int32, (WINDOW, WINDOW), 0)
    kj = lax.broadcasted_iota(jnp.int32, (WINDOW, WINDOW), 1)
    from_prev = jnp.concatenate([kj > qi] * 4, axis=0)

    def sink_rows(heads):
        return jnp.concatenate(
            [jnp.full((WINDOW, LANES), sinks_ref[hd] * LOG2E, jnp.float32) for hd in heads], axis=0)

    sink_a = sink_rows(GROUP_A_HEADS)
    sink_b = sink_rows(GROUP_B_HEADS)
    ones = jnp.ones((2 * WINDOW, LANES), jnp.bfloat16)

    def scores(qg, r0, kk, sink, no_prev):
        qs = jnp.concatenate(
            [qg[r0:r0 + WINDOW, sl * LANES:(sl + 1) * LANES] for sl in range(4)], axis=0)
        s = lax.dot_general(qs, kk, (((1,), (1,)), ((), ())),
                            preferred_element_type=jnp.float32)
        s_prev = s[:, :LANES] if no_prev is None else jnp.where(no_prev, NEG_INF, s[:, :LANES])
        s = jnp.where(from_prev, s_prev, s[:, LANES:])
        m = jnp.max(s, axis=-1, keepdims=True)
        m = jnp.maximum(jnp.broadcast_to(m, sink.shape), sink)
        return jnp.exp2(s - m), m

    def weighted(p, m, vv, sink):
        p = jnp.concatenate([jnp.where(from_prev, p, 0.0), jnp.where(from_prev, 0.0, p)],
                            axis=1).astype(jnp.bfloat16)
        ov = _dot(p, jnp.concatenate([vv, ones], axis=1))
        l = ov[:, LANES:] + jnp.exp2(sink - m)
        return ov[:, :LANES] / l

    def kv_window(j):
        if j == 0:
            return jnp.concatenate([kv_prev, kv[0:WINDOW]], axis=0)
        return kv[(j - 1) * WINDOW:(j + 1) * WINDOW]

    def block_scores(j):
        kvw = kv_window(j)
        no_prev = first if j == 0 else None
        return (scores(qa, j * WINDOW, kvw[:, 0:LANES], sink_a, no_prev),
                scores(qb, j * WINDOW, kvw[:, LANES:2 * LANES], sink_b, no_prev))

    def block_output(j, pm):
        kvw = kv_window(j)
        (p_a, m_a), (p_b, m_b) = pm
        o_a = weighted(p_a, m_a, kvw[:, 2 * LANES:3 * LANES], sink_a)
        o_b = weighted(p_b, m_b, kvw[:, 3 * LANES:4 * LANES], sink_b)
        rows = lambda o, sl: o[sl * WINDOW:(sl + 1) * WINDOW]
        y = jnp.concatenate([jnp.where(lo, rows(o_a, 0), rows(o_b, 0)),
                             jnp.where(lo, rows(o_a, 1), rows(o_b, 1)),
                             jnp.where(lo, rows(o_b, 2), rows(o_a, 2)),
                             jnp.where(lo, rows(o_b, 3), rows(o_a, 3))], axis=1)
        y = y * _rms_scale(y, ATTN_WIDTH) * ga_ref[...]
        ybuf[j * WINDOW:(j + 1) * WINDOW, 0:ATTN_WIDTH] = y.astype(jnp.bfloat16)

    n_blocks = tm // WINDOW
    mlp_pieces(MLP_PIECES_BEFORE_ATTENTION)
    pending = block_scores(0)
    for j in range(n_blocks):
        upcoming = block_scores(j + 1) if j + 1 < n_blocks else None
        mlp_pieces(MLP_PIECES_PER_BLOCK)
        block_output(j, pending)
        pending = upcoming
    x1 = x + _dot(ybuf[:, ATTN_WIDTH:], wo_ref[ATTN_WIDTH:, :])
    mlp_pieces(MLP_PIECES_BEFORE_OUT_PROJ)
    x1 = x1 + _dot(ybuf[:, :ATTN_WIDTH], wo_ref[:ATTN_WIDTH, :])
    mlp_pieces(len(pieces))

    x1_prev[...] = x1
    h_prev[...] = (x1 * _rms_scale(x1, D_MODEL) * g2_ref[...]).astype(jnp.bfloat16)


def _layer(x2, sinks, g1, w_in, gq, gk, conv_w, gc, ga, w_out, g2, w_up, w_down, seq_len):
    t = x2.shape[0]
    tm = TOKEN_TILE
    n_tiles = t // tm
    const = lambda shape: pl.BlockSpec(
        shape, lambda i, *_: (0,) * len(shape), pipeline_mode=pl.Buffered(1))
    grid_spec = pltpu.PrefetchScalarGridSpec(
        num_scalar_prefetch=1,
        grid=(n_tiles + 1,),
        in_specs=[
            pl.BlockSpec((tm, D_MODEL), lambda i, *_: (jnp.minimum(i, n_tiles - 1), 0)),
            const((1, D_MODEL)), const((1, HEAD_DIM)), const((1, HEAD_DIM)),
            pl.BlockSpec((None, CONV_K, CONV_WIDTH), lambda i, *_: (0, 0, 0),
                         pipeline_mode=pl.Buffered(1)),
            const((1, CONV_WIDTH)), const((1, ATTN_WIDTH)), const((1, D_MODEL)),
        ] + [pl.BlockSpec(memory_space=pl.ANY)] * 4,
        out_specs=pl.BlockSpec((tm, D_MODEL), lambda i, *_: (jnp.maximum(i - 1, 0), 0)),
        scratch_shapes=[
            pltpu.VMEM((tm, D_MODEL), jnp.float32),
            pltpu.VMEM((tm, D_MODEL), jnp.bfloat16),
            pltpu.VMEM((WINDOW, 4 * LANES), jnp.bfloat16),
            pltpu.VMEM((SUBLANES, CONV_WIDTH), jnp.float32),
            pltpu.VMEM((tm, D_MODEL), jnp.bfloat16),
            pltpu.VMEM((D_MODEL, IN_COLS), jnp.bfloat16),
            pltpu.VMEM((D_MODEL, D_MODEL), jnp.bfloat16),
            pltpu.VMEM((D_MODEL, D_FF), jnp.bfloat16),
            pltpu.VMEM((D_FF, D_MODEL), jnp.bfloat16),
            pltpu.VMEM((WEIGHT_STAGE_SLOTS,) + WEIGHT_STAGE_BLOCK, jnp.float32),
            pltpu.SemaphoreType.DMA((WEIGHT_STAGE_SLOTS,)),
        ],
    )
    return pl.pallas_call(
        functools.partial(_layer_kernel, seq_len // tm, n_tiles),
        out_shape=jax.ShapeDtypeStruct((t, D_MODEL), jnp.float32),
        grid_spec=grid_spec,
        compiler_params=pltpu.CompilerParams(
            dimension_semantics=("arbitrary",), vmem_limit_bytes=VMEM_LIMIT_BYTES),
        name="layer",
    )(sinks, x2, g1, gq, gk, conv_w, gc, ga, g2, w_in, w_out, w_up, w_down)


def kernel(x, attn_norm_g, w_in, q_norm_g, k_norm_g, sinks, conv_w, attn_out_g, conv_out_g,
           w_out, mlp_norm_g, w_up, w_down):
    b, s, d = x.shape
    depth = w_in.shape[0]
    assert d == D_MODEL and s % TOKEN_TILE == 0
    x2 = x.reshape(b * s, d)
    for l in range(depth):
        row = lambda p: p[l:l + 1]
        x2 = _layer(x2, sinks[l], row(attn_norm_g), w_in[l], row(q_norm_g), row(k_norm_g),
                    row(conv_w), row(conv_out_g), row(attn_out_g), w_out[l], row(mlp_norm_g),
                    w_up[l], w_down[l], s)
    return x2.reshape(b, s, d)
```
